```python
import math
import jax, jax.numpy as jnp
from jax import lax
import numpy as np

D_MODEL = 1024
BATCH = 4
SEQ = 8192
DEPTH = 1

HEAD_DIM = 64
V_DIM = 2 * HEAD_DIM
N_HEADS = D_MODEL // V_DIM
D_QK = 2 * N_HEADS * HEAD_DIM
D_V = N_HEADS * V_DIM
D_CONV = D_MODEL
CONV_WIDTH = 31
D_FF = 4 * D_MODEL
N_BRANCH = 2
N_MOD = 6
ROPE_THETA = 10000.0
Q_BLOCK = 128
EPS = 1e-6
D_IN = 2 * D_CONV + 2 * D_QK + D_V + N_BRANCH * D_MODEL

kernel_name = "hybrid_conformer_diffattn_block"


def rms_norm(x, g):
    xf = x.astype(jnp.float32)
    y = xf * lax.rsqrt(jnp.mean(xf * xf, axis=-1, keepdims=True) + EPS)
    return (y * g.astype(jnp.float32)).astype(x.dtype)


def layer_norm(x, g, b):
    xf = x.astype(jnp.float32)
    mu = jnp.mean(xf, axis=-1, keepdims=True)
    xc = xf - mu
    y = xc * lax.rsqrt(jnp.mean(xc * xc, axis=-1, keepdims=True) + EPS)
    return (y * g.astype(jnp.float32) + b.astype(jnp.float32)).astype(x.dtype)


def rope(x, pos):
    half = HEAD_DIM // 2
    inv_freq = ROPE_THETA ** (-jnp.arange(0, HEAD_DIM, 2, dtype=jnp.float32) / HEAD_DIM)
    ang = pos.astype(jnp.float32)[:, None] * inv_freq[None, :]
    cos = jnp.cos(ang)[None, :, None, :]
    sin = jnp.sin(ang)[None, :, None, :]
    xf = x.astype(jnp.float32)
    x1, x2 = xf[..., :half], xf[..., half:]
    return jnp.concatenate([x1 * cos - x2 * sin, x2 * cos + x1 * sin], axis=-1).astype(x.dtype)


def lambda_init(layer_idx):
    return 0.8 - 0.6 * math.exp(-0.3 * (layer_idx - 1))


def causal_diff_attention(q1, q2, k1, k2, v, lam):
    B, T, H, Dh = q1.shape
    nb = T // Q_BLOCK
    scale = 1.0 / math.sqrt(HEAD_DIM)
    kpos = jnp.arange(T)

    def to_blocks(q):
        return q.reshape(B, nb, Q_BLOCK, H, Dh).transpose(1, 0, 2, 3, 4)

    def one_block(args):
        i, qb1, qb2 = args
        qpos = i * Q_BLOCK + jnp.arange(Q_BLOCK)
        mask = kpos[None, :] <= qpos[:, None]

        def probs(qb, k):
            s = jnp.einsum('bqhd,bkhd->bhqk', qb, k, preferred_element_type=jnp.float32) * scale
            s = jnp.where(mask[None, None], s, -jnp.inf)
            return jax.nn.softmax(s, axis=-1)

        a = probs(qb1, k1) - lam * probs(qb2, k2)
        return jnp.einsum('bhqk,bkhe->bqhe', a.astype(v.dtype), v)

    out = lax.map(one_block, (jnp.arange(nb), to_blocks(q1), to_blocks(q2)))
    return out.transpose(1, 0, 2, 3, 4).reshape(B, T, H, V_DIM)


def setup_inputs(seed: int = 0) -> dict:
    key = jax.random.key(seed)
    ks = jax.random.split(key, 24)
    f32 = jnp.float32
    L = DEPTH

    def nrm(k, shape, s):
        return jax.random.normal(k, shape, f32) * s

    def gain(k, shape):
        return 1.0 + 0.02 * jax.random.normal(k, shape, f32)

    return {
        "x": jax.random.normal(ks[0], (BATCH, SEQ, D_MODEL), f32),
        "c": jax.random.normal(ks[1], (BATCH, D_MODEL), f32),
        "w_ada": nrm(ks[2], (L, D_MODEL, N_MOD * D_MODEL), D_MODEL ** -0.5),
        "b_ada": nrm(ks[3], (L, N_MOD * D_MODEL), 0.01),
        "pre_norm1": gain(ks[4], (L, D_MODEL)),
        "post_norm1": gain(ks[5], (L, D_MODEL)),
        "w_in": nrm(ks[6], (L, D_MODEL, D_IN), D_MODEL ** -0.5),
        "conv_w": nrm(ks[7], (L, CONV_WIDTH, D_CONV), CONV_WIDTH ** -0.5),
        "conv_b": nrm(ks[8], (L, D_CONV), 0.01),
        "conv_ln_g": gain(ks[9], (L, D_CONV)),
        "conv_ln_b": nrm(ks[10], (L, D_CONV), 0.01),
        "conv_w_out": nrm(ks[11], (L, D_CONV, D_MODEL), D_CONV ** -0.5),
        "conv_b_out": nrm(ks[12], (L, D_MODEL), 0.01),
        "lambda_q1": nrm(ks[13], (L, HEAD_DIM), 0.1),
        "lambda_k1": nrm(ks[14], (L, HEAD_DIM), 0.1),
        "lambda_q2": nrm(ks[15], (L, HEAD_DIM), 0.1),
        "lambda_k2": nrm(ks[16], (L, HEAD_DIM), 0.1),
        "head_norm": gain(ks[17], (L, V_DIM)),
        "w_o": nrm(ks[18], (L, D_MODEL, D_MODEL), D_MODEL ** -0.5),
        "pre_norm2": gain(ks[19], (L, D_MODEL)),
        "post_norm2": gain(ks[20], (L, D_MODEL)),
        "w_ff1": nrm(ks[21], (L, D_MODEL, D_FF), D_MODEL ** -0.5),
        "w_ff2": nrm(ks[22], (L, D_FF, D_MODEL), D_FF ** -0.5),
    }


def reference(x, c, w_ada, b_ada, pre_norm1, post_norm1, w_in, conv_w, conv_b, conv_ln_g,
              conv_ln_b, conv_w_out, conv_b_out, lambda_q1, lambda_k1, lambda_q2, lambda_k2,
              head_norm, w_o, pre_norm2, post_norm2, w_ff1, w_ff2):
    B, T, _ = x.shape
    pos = jnp.arange(T)
    cs = jax.nn.silu(c)
    for l in range(DEPTH):
        mod = (cs @ w_ada[l] + b_ada[l])[:, None, :]
        sh1, sc1, g1, sh2, sc2, g2 = jnp.split(mod, N_MOD, axis=-1)

        h = rms_norm(x, pre_norm1[l]) * (1.0 + sc1) + sh1
        proj = h @ w_in[l]
        o1 = 2 * D_CONV
        o2 = o1 + D_QK
        o3 = o2 + D_QK
        o4 = o3 + D_V
        u_glu, q, k, v, gate_logits = proj[..., :o1], proj[..., o1:o2], proj[..., o2:o3], proj[..., o3:o4], proj[..., o4:]

        ua, ub = jnp.split(u_glu, 2, axis=-1)
        u = ua * jax.nn.sigmoid(ub)
        u = lax.conv_general_dilated(
            u, conv_w[l][:, None, :].astype(u.dtype), window_strides=(1,),
            padding=[(CONV_WIDTH - 1, 0)], dimension_numbers=('NWC', 'WIO', 'NWC'),
            feature_group_count=D_CONV) + conv_b[l]
        u = jax.nn.silu(layer_norm(u, conv_ln_g[l], conv_ln_b[l]))
        y_conv = u @ conv_w_out[l] + conv_b_out[l]

        q = q.reshape(B, T, 2, N_HEADS, HEAD_DIM)
        k = k.reshape(B, T, 2, N_HEADS, HEAD_DIM)
        v = v.reshape(B, T, N_HEADS, V_DIM)
        q1, q2 = rope(q[:, :, 0], pos), rope(q[:, :, 1], pos)
        k1, k2 = rope(k[:, :, 0], pos), rope(k[:, :, 1], pos)
        lam_init = lambda_init(l + 1)
        lam = (jnp.exp(jnp.sum(lambda_q1[l].astype(jnp.float32) * lambda_k1[l].astype(jnp.float32)))
               - jnp.exp(jnp.sum(lambda_q2[l].astype(jnp.float32) * lambda_k2[l].astype(jnp.float32)))
               + lam_init)
        att = causal_diff_attention(q1, q2, k1, k2, v, lam)
        att = rms_norm(att, head_norm[l]) * (1.0 - lam_init)
        y_att = att.reshape(B, T, D_V)

        g_conv, g_att = jnp.split(jax.nn.sigmoid(gate_logits), N_BRANCH, axis=-1)
        y = (g_conv * y_conv + g_att * y_att) @ w_o[l]
        x = x + g1 * rms_norm(y, post_norm1[l])

        h = rms_norm(x, pre_norm2[l]) * (1.0 + sc2) + sh2
        f = jnp.square(jax.nn.relu(h @ w_ff1[l])) @ w_ff2[l]
        x = x + g2 * rms_norm(f, post_norm2[l])
    return x
```

```python
import functools
import math

import numpy as np
import jax
import jax.numpy as jnp
from jax import lax
from jax.experimental import pallas as pl
from jax.experimental.pallas import tpu as pltpu

F32 = jnp.float32
BF16 = jnp.bfloat16

D_MODEL = 1024
HEAD_DIM = 64
V_DIM = 2 * HEAD_DIM
N_HEADS = D_MODEL // V_DIM
CONV_WIDTH = 31
CONV_HALO = 32
D_FF = 4 * D_MODEL
N_MOD = 6
ROPE_THETA = 10000.0
EPS = 1e-6
LOG2E = 1.4426950408889634
LANES = 128
VMEM_LIMIT = 56 * 1024 * 1024


def _lambda_init(layer_idx):
    return 0.8 - 0.6 * math.exp(-0.3 * (layer_idx - 1))


def _rms(x, g):
    ms = jnp.mean(x * x, axis=-1, keepdims=True)
    return x * lax.rsqrt(ms + EPS) * g


def _dot(a, b):
    return jnp.dot(a, b, preferred_element_type=F32)


def _dot_nt(a, b):
    return lax.dot_general(a, b, (((1,), (1,)), ((), ())), preferred_element_type=F32)


def _const_spec(shape):
    return pl.BlockSpec(shape, lambda *_: (0,) * len(shape), pipeline_mode=pl.Buffered(1))


def _params(n_axes):
    return pltpu.CompilerParams(
        dimension_semantics=("arbitrary",) * n_axes, vmem_limit_bytes=VMEM_LIMIT)


def _mod_kernel(c_ref, w_ref, b_ref, o_ref):
    c = c_ref[...]
    cs = c * jax.nn.sigmoid(c)
    o_ref[...] = _dot(cs.astype(BF16), w_ref[...].astype(BF16)) + b_ref[...]


def _modulation(c_pad, w_ada, b_ada, tn=1024):
    rows = c_pad.shape[0]
    n = w_ada.shape[1]
    return pl.pallas_call(
        _mod_kernel,
        grid=(n // tn,),
        in_specs=[
            pl.BlockSpec((rows, D_MODEL), lambda j: (0, 0)),
            pl.BlockSpec((D_MODEL, tn), lambda j: (0, j)),
            pl.BlockSpec((1, tn), lambda j: (0, j)),
        ],
        out_specs=pl.BlockSpec((rows, tn), lambda j: (0, j)),
        out_shape=jax.ShapeDtypeStruct((rows, n), F32),
        compiler_params=_params(1),
        name="mod",
    )(c_pad, w_ada, b_ada)


def _in_proj_kernel(x_ref, mod_ref, pn_ref, cos_ref, sin_ref, wglu_ref, wq_ref, wk_ref, wvt_ref,
                    wg_ref, u_ref, q_ref, k_ref, vt_ref, g_ref, *, q_scale, cn):
    x = x_ref[0]
    sh1 = mod_ref[0, 0:1, :]
    sc1 = mod_ref[0, 1:2, :]
    h = _rms(x, pn_ref[...]) * (1.0 + sc1) + sh1
    hb = h.astype(BF16)

    for c in range(0, D_MODEL, cn):
        ua = _dot(hb, wglu_ref[:, c:c + cn])
        ub = _dot(hb, wglu_ref[:, D_MODEL + c:D_MODEL + c + cn])
        u_ref[0, :, c:c + cn] = ua * jax.nn.sigmoid(ub)

    cos = cos_ref[...]
    sin = sin_ref[...]
    for w_ref, o_ref, scale in ((wq_ref, q_ref, q_scale), (wk_ref, k_ref, None)):
        for c in range(0, D_MODEL, cn):
            y = _dot(hb, w_ref[:, c:c + cn])
            for g in range(0, cn, LANES):
                yg = y[:, g:g + LANES]
                r = yg * cos + pltpu.roll(yg, LANES // 2, axis=1) * sin
                if scale is not None:
                    r = r * scale
                o_ref[0, :, c + g:c + g + LANES] = r.astype(o_ref.dtype)

    for c in range(0, D_MODEL, cn):
        vt_ref[0, c:c + cn, :] = _dot_nt(wvt_ref[c:c + cn, :], hb).astype(vt_ref.dtype)

    for c in range(0, 2 * D_MODEL, cn):
        g_ref[0, :, c:c + cn] = jax.nn.sigmoid(_dot(hb, wg_ref[:, c:c + cn]))


def _in_proj(x, mod, pre_norm, cos_t, sin_t, wglu, wq, wk, wvt, wg, q_scale, tm=512, cn=512):
    b, t, _ = x.shape
    tm = min(tm, t)
    grid = (b, t // tm)
    row = lambda i, j: (i, j, 0)
    return pl.pallas_call(
        functools.partial(_in_proj_kernel, q_scale=q_scale, cn=cn),
        grid=grid,
        in_specs=[
            pl.BlockSpec((1, tm, D_MODEL), row),
            pl.BlockSpec((1, N_MOD, D_MODEL), lambda i, j: (i, 0, 0)),
            _const_spec((1, D_MODEL)),
            pl.BlockSpec((tm, LANES), lambda i, j: (j, 0)),
            pl.BlockSpec((tm, LANES), lambda i, j: (j, 0)),
            _const_spec((D_MODEL, 2 * D_MODEL)),
            _const_spec((D_MODEL, D_MODEL)),
            _const_spec((D_MODEL, D_MODEL)),
            _const_spec((D_MODEL, D_MODEL)),
            _const_spec((D_MODEL, 2 * D_MODEL)),
        ],
        out_specs=[
            pl.BlockSpec((1, tm, D_MODEL), row),
            pl.BlockSpec((1, tm, D_MODEL), row),
            pl.BlockSpec((1, tm, D_MODEL), row),
            pl.BlockSpec((1, D_MODEL, tm), lambda i, j: (i, 0, j)),
            pl.BlockSpec((1, tm, 2 * D_MODEL), row),
        ],
        out_shape=[
            jax.ShapeDtypeStruct((b, t, D_MODEL), F32),
            jax.ShapeDtypeStruct((b, t, D_MODEL), BF16),
            jax.ShapeDtypeStruct((b, t, D_MODEL), BF16),
            jax.ShapeDtypeStruct((b, D_MODEL, t), BF16),
            jax.ShapeDtypeStruct((b, t, 2 * D_MODEL), F32),
        ],
        compiler_params=_params(2),
        name="in_proj",
    )(x, mod, pre_norm, cos_t, sin_t, wglu, wq, wk, wvt, wg)


def _conv_kernel(u_ref, halo_ref, g_ref, cw_ref, cb_ref, lng_ref, lnb_ref, wo_ref, bo_ref,
                 o_ref, xp_ref, cv_ref, *, tm, rows, lanes):
    t = pl.program_id(1)
    halo = halo_ref[0]
    xp_ref[0:CONV_HALO, :] = jnp.where(t > 0, halo, jnp.zeros_like(halo))
    xp_ref[CONV_HALO:, :] = u_ref[0]

    first = CONV_HALO - (CONV_WIDTH - 1)
    win = rows + CONV_HALO

    def row_chunk(rc, carry):
        base = pl.multiple_of(rc * rows, rows)
        for lc in range(0, D_MODEL, lanes):
            w = xp_ref[pl.ds(base, win), lc:lc + lanes]
            acc = jnp.zeros((rows, lanes), F32)
            for j in range(CONV_WIDTH):
                shifted = pltpu.roll(w, win - (first + j), axis=0)[:rows]
                acc = acc + shifted * cw_ref[j:j + 1, lc:lc + lanes]
            cv_ref[pl.ds(base, rows), lc:lc + lanes] = acc
        return carry

    lax.fori_loop(0, tm // rows, row_chunk, 0)

    y = cv_ref[...] + cb_ref[...]
    mu = jnp.mean(y, axis=-1, keepdims=True)
    yc = y - mu
    var = jnp.mean(yc * yc, axis=-1, keepdims=True)
    z = yc * lax.rsqrt(var + EPS) * lng_ref[...] + lnb_ref[...]
    a = z * jax.nn.sigmoid(z)
    yo = _dot(a.astype(BF16), wo_ref[...]) + bo_ref[...]
    o_ref[0] = g_ref[0] * yo


def _conv_branch(u, g, conv_w, conv_b, ln_g, ln_b, w_out, b_out, tm=512, rows=32, lanes=256):
    b, t, _ = u.shape
    tm = min(tm, t)
    halo_blocks = tm // CONV_HALO
    row = lambda i, j: (i, j, 0)
    return pl.pallas_call(
        functools.partial(_conv_kernel, tm=tm, rows=rows, lanes=lanes),
        grid=(b, t // tm),
        in_specs=[
            pl.BlockSpec((1, tm, D_MODEL), row),
            pl.BlockSpec((1, CONV_HALO, D_MODEL),
                         lambda i, j: (i, jnp.maximum(j * halo_blocks - 1, 0), 0)),
            pl.BlockSpec((1, tm, D_MODEL), row),
            _const_spec((CONV_WIDTH, D_MODEL)),
            _const_spec((1, D_MODEL)),
            _const_spec((1, D_MODEL)),
            _const_spec((1, D_MODEL)),
            _const_spec((D_MODEL, D_MODEL)),
            _const_spec((1, D_MODEL)),
        ],
        out_specs=pl.BlockSpec((1, tm, D_MODEL), row),
        out_shape=jax.ShapeDtypeStruct((b, t, D_MODEL), F32),
        scratch_shapes=[
            pltpu.VMEM((tm + CONV_HALO, D_MODEL), F32),
            pltpu.VMEM((tm, D_MODEL), F32),
        ],
        compiler_params=_params(2),
        name="conv",
    )(u, u, g, conv_w, conv_b, ln_g, ln_b, w_out, b_out)


def _attn_kernel(q_ref, k_ref, vt_ref, lq1_ref, lk1_ref, lq2_ref, lk2_ref, hn_ref, o_ref,
                 qs_ref, m_ref, l_ref, acc_ref, *, tq, tk, cw, lam_init):
    i = pl.program_id(2)
    q = q_ref[0]
    lane = lax.broadcasted_iota(jnp.int32, (1, LANES), 1)
    map1 = (lane // (HEAD_DIM // 2)) % 2 == 0
    zero = jnp.zeros_like(q)
    qs_ref[0:tq, :] = jnp.where(map1, q, zero)
    qs_ref[tq:, :] = jnp.where(map1, zero, q)
    m_ref[...] = jnp.full(m_ref.shape, -jnp.inf, F32)
    l_ref[...] = jnp.zeros(l_ref.shape, F32)
    acc_ref[...] = jnp.zeros(acc_ref.shape, F32)

    def step(j, masked):
        start = pl.multiple_of(j * tk, tk)
        kb = k_ref[0, pl.ds(start, tk), :]
        vb = vt_ref[0, :, pl.ds(start, tk)]
        for c in range(0, 2 * tq, cw):
            s = _dot_nt(kb, qs_ref[c:c + cw, :])
            if masked:
                key = lax.broadcasted_iota(jnp.int32, (tk, cw), 0)
                qry = lax.broadcasted_iota(jnp.int32, (tk, cw), 1) + (c % tq)
                s = jnp.where(key <= qry, s, -jnp.inf)
            m_old = m_ref[:, c:c + cw]
            m_new = jnp.maximum(m_old, jnp.max(s, axis=0, keepdims=True))
            alpha = jnp.exp2(m_old - m_new)
            p = jnp.exp2(s - m_new)
            l_ref[:, c:c + cw] = alpha * l_ref[:, c:c + cw] + jnp.sum(p, axis=0, keepdims=True)
            acc_ref[:, c:c + cw] = acc_ref[:, c:c + cw] * alpha + _dot(vb, p.astype(BF16))
            m_ref[:, c:c + cw] = m_new

    def body(j, carry):
        step(j, masked=False)
        return carry

    lax.fori_loop(0, i, body, 0)
    step(i, masked=True)

    lam = (jnp.exp(jnp.sum(lq1_ref[...] * lk1_ref[...], axis=-1, keepdims=True))
           - jnp.exp(jnp.sum(lq2_ref[...] * lk2_ref[...], axis=-1, keepdims=True)) + lam_init)
    o1 = acc_ref[:, 0:tq] / l_ref[:, 0:tq]
    o2 = acc_ref[:, tq:] / l_ref[:, tq:]
    o = o1 - lam * o2
    ms = jnp.mean(o * o, axis=0, keepdims=True)
    on = (o * lax.rsqrt(ms + EPS)).T
    o_ref[0] = on * hn_ref[...] * (1.0 - lam_init)


def _attention(q, k, vt, lq1, lk1, lq2, lk2, head_norm, lam_init, tq=512, cw=256):
    b, t, _ = q.shape
    tq = min(tq, t)
    tk = tq
    vec = _const_spec((1, HEAD_DIM))
    return pl.pallas_call(
        functools.partial(_attn_kernel, tq=tq, tk=tk, cw=cw, lam_init=lam_init),
        grid=(b, N_HEADS, t // tq),
        in_specs=[
            pl.BlockSpec((1, tq, V_DIM), lambda bi, h, i: (bi, i, h)),
            pl.BlockSpec((1, t, V_DIM), lambda bi, h, i: (bi, 0, h)),
            pl.BlockSpec((1, V_DIM, t), lambda bi, h, i: (bi, h, 0)),
            vec, vec, vec, vec,
            _const_spec((1, V_DIM)),
        ],
        out_specs=pl.BlockSpec((1, tq, V_DIM), lambda bi, h, i: (bi, i, h)),
        out_shape=jax.ShapeDtypeStruct((b, t, D_MODEL), F32),
        scratch_shapes=[
            pltpu.VMEM((2 * tq, V_DIM), BF16),
            pltpu.VMEM((1, 2 * tq), F32),
            pltpu.VMEM((1, 2 * tq), F32),
            pltpu.VMEM((V_DIM, 2 * tq), F32),
        ],
        compiler_params=_params(3),
        name="attn",
    )(q, k, vt, lq1, lk1, lq2, lk2, head_norm)


def _out_mlp_kernel(x_ref, yc_ref, att_ref, g_ref, mod_ref, pn1_ref, pre2_ref, pn2_ref,
                    wo_ref, w1_ref, w2_ref, o_ref, *, fc):
    g1 = mod_ref[0, 2:3, :]
    sh2 = mod_ref[0, 3:4, :]
    sc2 = mod_ref[0, 4:5, :]
    g2 = mod_ref[0, 5:6, :]
    y = yc_ref[0] + g_ref[0] * att_ref[0]
    y = _dot(y.astype(BF16), wo_ref[...])
    x1 = x_ref[0] + g1 * _rms(y, pn1_ref[...])
    hb = (_rms(x1, pre2_ref[...]) * (1.0 + sc2) + sh2).astype(BF16)
    f = None
    for c in range(0, D_FF, fc):
        a = jnp.square(jnp.maximum(_dot(hb, w1_ref[:, c:c + fc]), 0.0))
        part = _dot(a.astype(BF16), w2_ref[c:c + fc, :])
        f = part if f is None else f + part
    o_ref[0] = x1 + g2 * _rms(f, pn2_ref[...])


def _out_mlp(x, yc, att, g, mod, post1, pre2, post2, w_o, w1, w2, tm=256, fc=1024):
    b, t, _ = x.shape
    tm = min(tm, t)
    row = lambda i, j: (i, j, 0)
    tile = pl.BlockSpec((1, tm, D_MODEL), row)
    return pl.pallas_call(
        functools.partial(_out_mlp_kernel, fc=fc),
        grid=(b, t // tm),
        in_specs=[
            tile, tile, tile,
            pl.BlockSpec((1, tm, D_MODEL), lambda i, j: (i, j, 1)),
            pl.BlockSpec((1, N_MOD, D_MODEL), lambda i, j: (i, 0, 0)),
            _const_spec((1, D_MODEL)),
            _const_spec((1, D_MODEL)),
            _const_spec((1, D_MODEL)),
            _const_spec((D_MODEL, D_MODEL)),
            _const_spec((D_MODEL, D_FF)),
            _const_spec((D_FF, D_MODEL)),
        ],
        out_specs=tile,
        out_shape=jax.ShapeDtypeStruct((b, t, D_MODEL), F32),
        compiler_params=_params(2),
        name="out_mlp",
    )(x, yc, att, g, mod, post1, pre2, post2, w_o, w1, w2)


def _qk_column_order():
    idx = np.empty((N_HEADS, 4, HEAD_DIM // 2), np.int32)
    for h in range(N_HEADS):
        for seg in range(4):
            m, half = seg % 2, seg // 2
            idx[h, seg] = m * N_HEADS * HEAD_DIM + h * HEAD_DIM + half * (HEAD_DIM // 2) \
                + np.arange(HEAD_DIM // 2)
    return idx.reshape(-1)


def _rope_tables(t):
    half = HEAD_DIM // 2
    inv_freq = ROPE_THETA ** (-jnp.arange(0, HEAD_DIM, 2, dtype=F32) / HEAD_DIM)
    ang = jnp.arange(t).astype(F32)[:, None] * inv_freq[None, :]
    cos, sin = jnp.cos(ang), jnp.sin(ang)
    return jnp.tile(cos, (1, LANES // half)), jnp.concatenate([-sin, -sin, sin, sin], axis=1)


def kernel(x, c, w_ada, b_ada, pre_norm1, post_norm1, w_in, conv_w, conv_b, conv_ln_g, conv_ln_b,
           conv_w_out, conv_b_out, lambda_q1, lambda_k1, lambda_q2, lambda_k2, head_norm, w_o,
           pre_norm2, post_norm2, w_ff1, w_ff2):
    b, t, _ = x.shape
    depth = w_in.shape[0]
    d_qk = 2 * N_HEADS * HEAD_DIM
    o1 = 2 * D_MODEL
    o2 = o1 + d_qk
    o3 = o2 + d_qk
    o4 = o3 + D_MODEL
    order = _qk_column_order()
    cos_t, sin_t = _rope_tables(t)
    c_pad = jnp.pad(c, ((0, (-b) % 8), (0, 0)))
    row = lambda v: v.reshape(1, -1)

    for l in range(depth):
        mod = _modulation(c_pad, w_ada[l], row(b_ada[l]))[:b].reshape(b, N_MOD, D_MODEL)
        wl = w_in[l]
        wglu = wl[:, :o1].astype(BF16)
        wq = jnp.take(wl[:, o1:o2], order, axis=1).astype(BF16)
        wk = jnp.take(wl[:, o2:o3], order, axis=1).astype(BF16)
        wvt = wl[:, o3:o4].T.astype(BF16)
        wg = wl[:, o4:].astype(BF16)
        q_scale = LOG2E / math.sqrt(HEAD_DIM)
        u, q, k, vt, g = _in_proj(x, mod, row(pre_norm1[l]), cos_t, sin_t, wglu, wq, wk, wvt, wg,
                                  q_scale)
        yc = _conv_branch(u, g, conv_w[l], row(conv_b[l]), row(conv_ln_g[l]), row(conv_ln_b[l]),
                          conv_w_out[l].astype(BF16), row(conv_b_out[l]))
        att = _attention(q, k, vt, row(lambda_q1[l]), row(lambda_k1[l]), row(lambda_q2[l]),
                         row(lambda_k2[l]), row(head_norm[l]), _lambda_init(l + 1))
        x = _out_mlp(x, yc, att, g, mod, row(post_norm1[l]), row(pre_norm2[l]), row(post_norm2[l]),
                     w_o[l].astype(BF16), w_ff1[l].astype(BF16), w_ff2[l].astype(BF16))
    return x
```

```python
import functools
import math

import numpy as np
import jax
import jax.numpy as jnp
from jax import lax
from jax.experimental import pallas as pl
from jax.experimental.pallas import tpu as pltpu

F32 = jnp.float32
BF16 = jnp.bfloat16

D_MODEL = 1024
HEAD_DIM = 64
V_DIM = 2 * HEAD_DIM
N_HEADS = D_MODEL // V_DIM
CONV_WIDTH = 31
CONV_HALO = 32
D_FF = 4 * D_MODEL
N_MOD = 6
ROPE_THETA = 10000.0
EPS = 1e-6
LOG2E = 1.4426950408889634
LANES = 128
VMEM_LIMIT = 56 * 1024 * 1024


def _lambda_init(layer_idx):
    return 0.8 - 0.6 * math.exp(-0.3 * (layer_idx - 1))


def _rms(x, g):
    ms = jnp.mean(x * x, axis=-1, keepdims=True)
    return x * lax.rsqrt(ms + EPS) * g


def _dot(a, b):
    return jnp.dot(a, b, preferred_element_type=F32)


def _dot_nt(a, b):
    return lax.dot_general(a, b, (((1,), (1,)), ((), ())), preferred_element_type=F32)


def _const_spec(shape):
    return pl.BlockSpec(shape, lambda *_: (0,) * len(shape), pipeline_mode=pl.Buffered(1))


def _params(n_axes):
    return pltpu.CompilerParams(
        dimension_semantics=("arbitrary",) * n_axes, vmem_limit_bytes=VMEM_LIMIT)


def _mod_kernel(c_ref, w_ref, b_ref, o_ref):
    c = c_ref[...]
    cs = c * jax.nn.sigmoid(c)
    o_ref[...] = _dot(cs.astype(BF16), w_ref[...].astype(BF16)) + b_ref[...]


def _modulation(c_pad, w_ada, b_ada, tn=1024):
    rows = c_pad.shape[0]
    n = w_ada.shape[1]
    return pl.pallas_call(
        _mod_kernel,
        grid=(n // tn,),
        in_specs=[
            pl.BlockSpec((rows, D_MODEL), lambda j: (0, 0)),
            pl.BlockSpec((D_MODEL, tn), lambda j: (0, j)),
            pl.BlockSpec((1, tn), lambda j: (0, j)),
        ],
        out_specs=pl.BlockSpec((rows, tn), lambda j: (0, j)),
        out_shape=jax.ShapeDtypeStruct((rows, n), F32),
        compiler_params=_params(1),
        name="mod",
    )(c_pad, w_ada, b_ada)


def _in_proj_kernel(x_ref, mod_ref, pn_ref, cos_ref, sin_ref, wglu_ref, wq_ref, wk_ref, wvt_ref,
                    wg_ref, u_ref, q_ref, k_ref, vt_ref, g_ref, *, q_scale, cn):
    x = x_ref[0]
    sh1 = mod_ref[0, 0:1, :]
    sc1 = mod_ref[0, 1:2, :]
    h = _rms(x, pn_ref[...]) * (1.0 + sc1) + sh1
    hb = h.astype(BF16)

    for c in range(0, D_MODEL, cn):
        ua = _dot(hb, wglu_ref[:, c:c + cn])
        ub = _dot(hb, wglu_ref[:, D_MODEL + c:D_MODEL + c + cn])
        u_ref[0, :, c:c + cn] = ua * jax.nn.sigmoid(ub)

    cos = cos_ref[...]
    sin = sin_ref[...]
    for w_ref, o_ref, scale in ((wq_ref, q_ref, q_scale), (wk_ref, k_ref, None)):
        for c in range(0, D_MODEL, cn):
            y = _dot(hb, w_ref[:, c:c + cn])
            for g in range(0, cn, LANES):
                yg = y[:, g:g + LANES]
                r = yg * cos + pltpu.roll(yg, LANES // 2, axis=1) * sin
                if scale is not None:
                    r = r * scale
                o_ref[0, :, c + g:c + g + LANES] = r.astype(o_ref.dtype)

    for c in range(0, D_MODEL, cn):
        vt_ref[0, c:c + cn, :] = _dot_nt(wvt_ref[c:c + cn, :], hb).astype(vt_ref.dtype)

    for c in range(0, 2 * D_MODEL, cn):
        g_ref[0, :, c:c + cn] = jax.nn.sigmoid(_dot(hb, wg_ref[:, c:c + cn]))


def _in_proj(x, mod, pre_norm, cos_t, sin_t, wglu, wq, wk, wvt, wg, q_scale, tm=512, cn=512):
    b, t, _ = x.shape
    tm = min(tm, t)
    grid = (b, t // tm)
    row = lambda i, j: (i, j, 0)
    return pl.pallas_call(
        functools.partial(_in_proj_kernel, q_scale=q_scale, cn=cn),
        grid=grid,
        in_specs=[
            pl.BlockSpec((1, tm, D_MODEL), row),
            pl.BlockSpec((1, N_MOD, D_MODEL), lambda i, j: (i, 0, 0)),
            _const_spec((1, D_MODEL)),
            pl.BlockSpec((tm, LANES), lambda i, j: (j, 0)),
            pl.BlockSpec((tm, LANES), lambda i, j: (j, 0)),
            _const_spec((D_MODEL, 2 * D_MODEL)),
            _const_spec((D_MODEL, D_MODEL)),
            _const_spec((D_MODEL, D_MODEL)),
            _const_spec((D_MODEL, D_MODEL)),
            _const_spec((D_MODEL, 2 * D_MODEL)),
        ],
        out_specs=[
            pl.BlockSpec((1, tm, D_MODEL), row),
            pl.BlockSpec((1, tm, D_MODEL), row),
            pl.BlockSpec((1, tm, D_MODEL), row),
            pl.BlockSpec((1, D_MODEL, tm), lambda i, j: (i, 0, j)),
            pl.BlockSpec((1, tm, 2 * D_MODEL), row),
        ],
        out_shape=[
            jax.ShapeDtypeStruct((b, t, D_MODEL), F32),
            jax.ShapeDtypeStruct((b, t, D_MODEL), BF16),
            jax.ShapeDtypeStruct((b, t, D_MODEL), BF16),
            jax.ShapeDtypeStruct((b, D_MODEL, t), BF16),
            jax.ShapeDtypeStruct((b, t, 2 * D_MODEL), F32),
        ],
        compiler_params=_params(2),
        name="in_proj",
    )(x, mod, pre_norm, cos_t, sin_t, wglu, wq, wk, wvt, wg)


def _conv_kernel(u_ref, halo_ref, g_ref, cw_ref, cb_ref, lng_ref, lnb_ref, wo_ref, bo_ref,
                 o_ref, xp_ref, cv_ref, *, tm, rows, lanes):
    t = pl.program_id(1)
    halo = halo_ref[0]
    xp_ref[0:CONV_HALO, :] = jnp.where(t > 0, halo, jnp.zeros_like(halo))
    xp_ref[CONV_HALO:, :] = u_ref[0]

    first = CONV_HALO - (CONV_WIDTH - 1)
    win = rows + CONV_HALO

    def row_chunk(rc, carry):
        base = pl.multiple_of(rc * rows, rows)
        for lc in range(0, D_MODEL, lanes):
            w = xp_ref[pl.ds(base, win), lc:lc + lanes]
            acc = jnp.zeros((rows, lanes), F32)
            for j in range(CONV_WIDTH):
                shifted = pltpu.roll(w, win - (first + j), axis=0)[:rows]
                acc = acc + shifted * cw_ref[j:j + 1, lc:lc + lanes]
            cv_ref[pl.ds(base, rows), lc:lc + lanes] = acc
        return carry

    lax.fori_loop(0, tm // rows, row_chunk, 0)

    y = cv_ref[...] + cb_ref[...]
    mu = jnp.mean(y, axis=-1, keepdims=True)
    yc = y - mu
    var = jnp.mean(yc * yc, axis=-1, keepdims=True)
    z = yc * lax.rsqrt(var + EPS) * lng_ref[...] + lnb_ref[...]
    a = z * jax.nn.sigmoid(z)
    yo = _dot(a.astype(BF16), wo_ref[...]) + bo_ref[...]
    o_ref[0] = g_ref[0] * yo


def _conv_branch(u, g, conv_w, conv_b, ln_g, ln_b, w_out, b_out, tm=512, rows=32, lanes=256):
    b, t, _ = u.shape
    tm = min(tm, t)
    halo_blocks = tm // CONV_HALO
    row = lambda i, j: (i, j, 0)
    return pl.pallas_call(
        functools.partial(_conv_kernel, tm=tm, rows=rows, lanes=lanes),
        grid=(b, t // tm),
        in_specs=[
            pl.BlockSpec((1, tm, D_MODEL), row),
            pl.BlockSpec((1, CONV_HALO, D_MODEL),
                         lambda i, j: (i, jnp.maximum(j * halo_blocks - 1, 0), 0)),
            pl.BlockSpec((1, tm, D_MODEL), row),
            _const_spec((CONV_WIDTH, D_MODEL)),
            _const_spec((1, D_MODEL)),
            _const_spec((1, D_MODEL)),
            _const_spec((1, D_MODEL)),
            _const_spec((D_MODEL, D_MODEL)),
            _const_spec((1, D_MODEL)),
        ],
        out_specs=pl.BlockSpec((1, tm, D_MODEL), row),
        out_shape=jax.ShapeDtypeStruct((b, t, D_MODEL), F32),
        scratch_shapes=[
            pltpu.VMEM((tm + CONV_HALO, D_MODEL), F32),
            pltpu.VMEM((tm, D_MODEL), F32),
        ],
        compiler_params=_params(2),
        name="conv",
    )(u, u, g, conv_w, conv_b, ln_g, ln_b, w_out, b_out)


def _attn_kernel(q_ref, k_ref, vt_ref, lq1_ref, lk1_ref, lq2_ref, lk2_ref, hn_ref, o_ref,
                 qs_ref, s_ref, p_ref, cm_ref, m_ref, l_ref, al_ref, acc_ref,
                 *, tq, tk, cw, lam_init):
    i = pl.program_id(2)
    q = q_ref[0]
    lane = lax.broadcasted_iota(jnp.int32, (1, LANES), 1)
    map1 = (lane // (HEAD_DIM // 2)) % 2 == 0
    zero = jnp.zeros_like(q)
    qs_ref[0:tq, :] = jnp.where(map1, q, zero)
    qs_ref[tq:, :] = jnp.where(map1, zero, q)
    m_ref[...] = jnp.full(m_ref.shape, -jnp.inf, F32)
    l_ref[...] = jnp.zeros(l_ref.shape, F32)
    al_ref[...] = jnp.ones(al_ref.shape, F32)
    acc_ref[...] = jnp.zeros(acc_ref.shape, F32)
    p_ref[...] = jnp.zeros(p_ref.shape, BF16)
    chunks = range(0, 2 * tq, cw)

    def scores(j):
        start = pl.multiple_of(j * tk, tk)
        kb = k_ref[0, pl.ds(start, tk), :]
        for c in chunks:
            s = _dot_nt(kb, qs_ref[c:c + cw, :])
            s_ref[:, c:c + cw] = s
            cm_ref[:, c:c + cw] = jnp.max(s, axis=0, keepdims=True)

    def values(j):
        start = pl.multiple_of(jnp.maximum(j, 0) * tk, tk)
        vb = vt_ref[0, :, pl.ds(start, tk)]
        for c in chunks:
            acc_ref[:, c:c + cw] = (acc_ref[:, c:c + cw] * al_ref[:, c:c + cw]
                                    + _dot(vb, p_ref[:, c:c + cw]))

    def softmax(masked):
        for c in chunks:
            s = s_ref[:, c:c + cw]
            if masked:
                key = lax.broadcasted_iota(jnp.int32, (tk, cw), 0)
                qry = lax.broadcasted_iota(jnp.int32, (tk, cw), 1) + (c % tq)
                s = jnp.where(key <= qry, s, -jnp.inf)
                cm = jnp.max(s, axis=0, keepdims=True)
            else:
                cm = cm_ref[:, c:c + cw]
            m_old = m_ref[:, c:c + cw]
            m_new = jnp.maximum(m_old, cm)
            alpha = jnp.exp2(m_old - m_new)
            p = jnp.exp2(s - m_new)
            l_ref[:, c:c + cw] = alpha * l_ref[:, c:c + cw] + jnp.sum(p, axis=0, keepdims=True)
            m_ref[:, c:c + cw] = m_new
            al_ref[:, c:c + cw] = alpha
            p_ref[:, c:c + cw] = p.astype(BF16)

    scores(0)

    def body(j, carry):
        values(j - 1)
        softmax(masked=False)
        scores(j + 1)
        return carry

    lax.fori_loop(0, i, body, 0)
    values(i - 1)
    softmax(masked=True)
    values(i)

    lam = (jnp.exp(jnp.sum(lq1_ref[...] * lk1_ref[...], axis=-1, keepdims=True))
           - jnp.exp(jnp.sum(lq2_ref[...] * lk2_ref[...], axis=-1, keepdims=True)) + lam_init)
    o1 = acc_ref[:, 0:tq] / l_ref[:, 0:tq]
    o2 = acc_ref[:, tq:] / l_ref[:, tq:]
    o = o1 - lam * o2
    ms = jnp.mean(o * o, axis=0, keepdims=True)
    on = (o * lax.rsqrt(ms + EPS)).T
    o_ref[0] = on * hn_ref[...] * (1.0 - lam_init)


def _attention(q, k, vt, lq1, lk1, lq2, lk2, head_norm, lam_init, tq=512, cw=256):
    b, t, _ = q.shape
    tq = min(tq, t)
    tk = tq
    vec = _const_spec((1, HEAD_DIM))
    stat = pltpu.VMEM((1, 2 * tq), F32)
    return pl.pallas_call(
        functools.partial(_attn_kernel, tq=tq, tk=tk, cw=cw, lam_init=lam_init),
        grid=(b, N_HEADS, t // tq),
        in_specs=[
            pl.BlockSpec((1, tq, V_DIM), lambda bi, h, i: (bi, i, h)),
            pl.BlockSpec((1, t, V_DIM), lambda bi, h, i: (bi, 0, h)),
            pl.BlockSpec((1, V_DIM, t), lambda bi, h, i: (bi, h, 0)),
            vec, vec, vec, vec,
            _const_spec((1, V_DIM)),
        ],
        out_specs=pl.BlockSpec((1, tq, V_DIM), lambda bi, h, i: (bi, i, h)),
        out_shape=jax.ShapeDtypeStruct((b, t, D_MODEL), F32),
        scratch_shapes=[
            pltpu.VMEM((2 * tq, V_DIM), BF16),
            pltpu.VMEM((tk, 2 * tq), F32),
            pltpu.VMEM((tk, 2 * tq), BF16),
            stat, stat, stat, stat,
            pltpu.VMEM((V_DIM, 2 * tq), F32),
        ],
        compiler_params=_params(3),
        name="attn",
    )(q, k, vt, lq1, lk1, lq2, lk2, head_norm)


def _out_mlp_kernel(x_ref, yc_ref, att_ref, g_ref, mod_ref, pn1_ref, pre2_ref, pn2_ref,
                    wo_ref, w1_ref, w2_ref, o_ref, *, fc):
    g1 = mod_ref[0, 2:3, :]
    sh2 = mod_ref[0, 3:4, :]
    sc2 = mod_ref[0, 4:5, :]
    g2 = mod_ref[0, 5:6, :]
    y = yc_ref[0] + g_ref[0] * att_ref[0]
    y = _dot(y.astype(BF16), wo_ref[...])
    x1 = x_ref[0] + g1 * _rms(y, pn1_ref[...])
    hb = (_rms(x1, pre2_ref[...]) * (1.0 + sc2) + sh2).astype(BF16)
    f = None
    for c in range(0, D_FF, fc):
        a = jnp.square(jnp.maximum(_dot(hb, w1_ref[:, c:c + fc]), 0.0))
        part = _dot(a.astype(BF16), w2_ref[c:c + fc, :])
        f = part if f is None else f + part
    o_ref[0] = x1 + g2 * _rms(f, pn2_ref[...])


def _out_mlp(x, yc, att, g, mod, post1, pre2, post2, w_o, w1, w2, tm=256, fc=1024):
    b, t, _ = x.shape
    tm = min(tm, t)
    row = lambda i, j: (i, j, 0)
    tile = pl.BlockSpec((1, tm, D_MODEL), row)
    return pl.pallas_call(
        functools.partial(_out_mlp_kernel, fc=fc),
        grid=(b, t // tm),
        in_specs=[
            tile, tile, tile,
            pl.BlockSpec((1, tm, D_MODEL), lambda i, j: (i, j, 1)),
            pl.BlockSpec((1, N_MOD, D_MODEL), lambda i, j: (i, 0, 0)),
            _const_spec((1, D_MODEL)),
            _const_spec((1, D_MODEL)),
            _const_spec((1, D_MODEL)),
            _const_spec((D_MODEL, D_MODEL)),
            _const_spec((D_MODEL, D_FF)),
            _const_spec((D_FF, D_MODEL)),
        ],
        out_specs=tile,
        out_shape=jax.ShapeDtypeStruct((b, t, D_MODEL), F32),
        compiler_params=_params(2),
        name="out_mlp",
    )(x, yc, att, g, mod, post1, pre2, post2, w_o, w1, w2)


def _qk_column_order():
    idx = np.empty((N_HEADS, 4, HEAD_DIM // 2), np.int32)
    for h in range(N_HEADS):
        for seg in range(4):
            m, half = seg % 2, seg // 2
            idx[h, seg] = m * N_HEADS * HEAD_DIM + h * HEAD_DIM + half * (HEAD_DIM // 2) \
                + np.arange(HEAD_DIM // 2)
    return idx.reshape(-1)


def _rope_tables(t):
    half = HEAD_DIM // 2
    inv_freq = ROPE_THETA ** (-jnp.arange(0, HEAD_DIM, 2, dtype=F32) / HEAD_DIM)
    ang = jnp.arange(t).astype(F32)[:, None] * inv_freq[None, :]
    cos, sin = jnp.cos(ang), jnp.sin(ang)
    return jnp.tile(cos, (1, LANES // half)), jnp.concatenate([-sin, -sin, sin, sin], axis=1)


def kernel(x, c, w_ada, b_ada, pre_norm1, post_norm1, w_in, conv_w, conv_b, conv_ln_g, conv_ln_b,
           conv_w_out, conv_b_out, lambda_q1, lambda_k1, lambda_q2, lambda_k2, head_norm, w_o,
           pre_norm2, post_norm2, w_ff1, w_ff2):
    b, t, _ = x.shape
    depth = w_in.shape[0]
    d_qk = 2 * N_HEADS * HEAD_DIM
    o1 = 2 * D_MODEL
    o2 = o1 + d_qk
    o3 = o2 + d_qk
    o4 = o3 + D_MODEL
    order = _qk_column_order()
    cos_t, sin_t = _rope_tables(t)
    c_pad = jnp.pad(c, ((0, (-b) % 8), (0, 0)))
    row = lambda v: v.reshape(1, -1)

    for l in range(depth):
        mod = _modulation(c_pad, w_ada[l], row(b_ada[l]))[:b].reshape(b, N_MOD, D_MODEL)
        wl = w_in[l]
        wglu = wl[:, :o1].astype(BF16)
        wq = jnp.take(wl[:, o1:o2], order, axis=1).astype(BF16)
        wk = jnp.take(wl[:, o2:o3], order, axis=1).astype(BF16)
        wvt = wl[:, o3:o4].T.astype(BF16)
        wg = wl[:, o4:].astype(BF16)
        q_scale = LOG2E / math.sqrt(HEAD_DIM)
        u, q, k, vt, g = _in_proj(x, mod, row(pre_norm1[l]), cos_t, sin_t, wglu, wq, wk, wvt, wg,
                                  q_scale)
        yc = _conv_branch(u, g, conv_w[l], row(conv_b[l]), row(conv_ln_g[l]), row(conv_ln_b[l]),
                          conv_w_out[l].astype(BF16), row(conv_b_out[l]))
        att = _attention(q, k, vt, row(lambda_q1[l]), row(lambda_k1[l]), row(lambda_q2[l]),
                         row(lambda_k2[l]), row(head_norm[l]), _lambda_init(l + 1))
        x = _out_mlp(x, yc, att, g, mod, row(post_norm1[l]), row(pre_norm2[l]), row(post_norm2[l]),
                     w_o[l].astype(BF16), w_ff1[l].astype(BF16), w_ff2[l].astype(BF16))
    return x
```

```python
import functools
import math

import numpy as np
import jax
import jax.numpy as jnp
from jax import lax
from jax.experimental import pallas as pl
from jax.experimental.pallas import tpu as pltpu

F32 = jnp.float32
BF16 = jnp.bfloat16

D_MODEL = 1024
HEAD_DIM = 64
V_DIM = 2 * HEAD_DIM
N_HEADS = D_MODEL // V_DIM
CONV_WIDTH = 31
CONV_HALO = 32
D_FF = 4 * D_MODEL
N_MOD = 6
ROPE_THETA = 10000.0
EPS = 1e-6
LOG2E = 1.4426950408889634
LANES = 128
VMEM_LIMIT = 56 * 1024 * 1024


def _lambda_init(layer_idx):
    return 0.8 - 0.6 * math.exp(-0.3 * (layer_idx - 1))


def _rms(x, g):
    ms = jnp.mean(x * x, axis=-1, keepdims=True)
    return x * lax.rsqrt(ms + EPS) * g


def _dot(a, b):
    return jnp.dot(a, b, preferred_element_type=F32)


def _dot_nt(a, b):
    return lax.dot_general(a, b, (((1,), (1,)), ((), ())), preferred_element_type=F32)


def _const_spec(shape):
    return pl.BlockSpec(shape, lambda *_: (0,) * len(shape), pipeline_mode=pl.Buffered(1))


def _params(n_axes):
    return pltpu.CompilerParams(
        dimension_semantics=("arbitrary",) * n_axes, vmem_limit_bytes=VMEM_LIMIT)


def _mod_kernel(c_ref, w_ref, b_ref, o_ref):
    c = c_ref[...]
    cs = c * jax.nn.sigmoid(c)
    o_ref[...] = _dot(cs.astype(BF16), w_ref[...].astype(BF16)) + b_ref[...]


def _modulation(c_pad, w_ada, b_ada, tn=1024):
    rows = c_pad.shape[0]
    n = w_ada.shape[1]
    return pl.pallas_call(
        _mod_kernel,
        grid=(n // tn,),
        in_specs=[
            pl.BlockSpec((rows, D_MODEL), lambda j: (0, 0)),
            pl.BlockSpec((D_MODEL, tn), lambda j: (0, j)),
            pl.BlockSpec((1, tn), lambda j: (0, j)),
        ],
        out_specs=pl.BlockSpec((rows, tn), lambda j: (0, j)),
        out_shape=jax.ShapeDtypeStruct((rows, n), F32),
        compiler_params=_params(1),
        name="mod",
    )(c_pad, w_ada, b_ada)


def _in_proj_kernel(x_ref, mod_ref, pn_ref, cos_ref, sin_ref, wglu_ref, wq_ref, wk_ref, wvt_ref,
                    wg_ref, u_ref, q_ref, k_ref, vt_ref, g_ref, *, q_scale, cn):
    x = x_ref[0]
    sh1 = mod_ref[0, 0:1, :]
    sc1 = mod_ref[0, 1:2, :]
    h = _rms(x, pn_ref[...]) * (1.0 + sc1) + sh1
    hb = h.astype(BF16)

    for c in range(0, D_MODEL, cn):
        ua = _dot(hb, wglu_ref[:, c:c + cn])
        ub = _dot(hb, wglu_ref[:, D_MODEL + c:D_MODEL + c + cn])
        u_ref[0, :, c:c + cn] = ua * jax.nn.sigmoid(ub)

    cos = cos_ref[...]
    sin = sin_ref[...]
    for w_ref, o_ref, scale in ((wq_ref, q_ref, q_scale), (wk_ref, k_ref, None)):
        for c in range(0, D_MODEL, cn):
            y = _dot(hb, w_ref[:, c:c + cn])
            for g in range(0, cn, LANES):
                yg = y[:, g:g + LANES]
                r = yg * cos + pltpu.roll(yg, LANES // 2, axis=1) * sin
                if scale is not None:
                    r = r * scale
                o_ref[0, :, c + g:c + g + LANES] = r.astype(o_ref.dtype)

    for c in range(0, D_MODEL, cn):
        vt_ref[0, c:c + cn, :] = _dot_nt(wvt_ref[c:c + cn, :], hb).astype(vt_ref.dtype)

    for c in range(0, 2 * D_MODEL, cn):
        g_ref[0, :, c:c + cn] = jax.nn.sigmoid(_dot(hb, wg_ref[:, c:c + cn]))


def _in_proj(x, mod, pre_norm, cos_t, sin_t, wglu, wq, wk, wvt, wg, q_scale, tm=512, cn=512):
    b, t, _ = x.shape
    tm = min(tm, t)
    grid = (b, t // tm)
    row = lambda i, j: (i, j, 0)
    return pl.pallas_call(
        functools.partial(_in_proj_kernel, q_scale=q_scale, cn=cn),
        grid=grid,
        in_specs=[
            pl.BlockSpec((1, tm, D_MODEL), row),
            pl.BlockSpec((1, N_MOD, D_MODEL), lambda i, j: (i, 0, 0)),
            _const_spec((1, D_MODEL)),
            pl.BlockSpec((tm, LANES), lambda i, j: (j, 0)),
            pl.BlockSpec((tm, LANES), lambda i, j: (j, 0)),
            _const_spec((D_MODEL, 2 * D_MODEL)),
            _const_spec((D_MODEL, D_MODEL)),
            _const_spec((D_MODEL, D_MODEL)),
            _const_spec((D_MODEL, D_MODEL)),
            _const_spec((D_MODEL, 2 * D_MODEL)),
        ],
        out_specs=[
            pl.BlockSpec((1, tm, D_MODEL), row),
            pl.BlockSpec((1, tm, D_MODEL), row),
            pl.BlockSpec((1, tm, D_MODEL), row),
            pl.BlockSpec((1, D_MODEL, tm), lambda i, j: (i, 0, j)),
            pl.BlockSpec((1, tm, 2 * D_MODEL), row),
        ],
        out_shape=[
            jax.ShapeDtypeStruct((b, t, D_MODEL), F32),
            jax.ShapeDtypeStruct((b, t, D_MODEL), BF16),
            jax.ShapeDtypeStruct((b, t, D_MODEL), BF16),
            jax.ShapeDtypeStruct((b, D_MODEL, t), BF16),
            jax.ShapeDtypeStruct((b, t, 2 * D_MODEL), F32),
        ],
        compiler_params=_params(2),
        name="in_proj",
    )(x, mod, pre_norm, cos_t, sin_t, wglu, wq, wk, wvt, wg)


def _conv_kernel(u_ref, halo_ref, g_ref, cw_ref, cb_ref, lng_ref, lnb_ref, wo_ref, bo_ref,
                 o_ref, xp_ref, cv_ref, *, tm, rows, lanes):
    t = pl.program_id(1)
    halo = halo_ref[0]
    xp_ref[0:CONV_HALO, :] = jnp.where(t > 0, halo, jnp.zeros_like(halo))
    xp_ref[CONV_HALO:, :] = u_ref[0]

    first = CONV_HALO - (CONV_WIDTH - 1)
    win = rows + CONV_HALO

    def row_chunk(rc, carry):
        base = pl.multiple_of(rc * rows, rows)
        for lc in range(0, D_MODEL, lanes):
            w = xp_ref[pl.ds(base, win), lc:lc + lanes]
            acc = jnp.zeros((rows, lanes), F32)
            for j in range(CONV_WIDTH):
                shifted = pltpu.roll(w, win - (first + j), axis=0)[:rows]
                acc = acc + shifted * cw_ref[j:j + 1, lc:lc + lanes]
            cv_ref[pl.ds(base, rows), lc:lc + lanes] = acc
        return carry

    lax.fori_loop(0, tm // rows, row_chunk, 0)

    y = cv_ref[...] + cb_ref[...]
    mu = jnp.mean(y, axis=-1, keepdims=True)
    yc = y - mu
    var = jnp.mean(yc * yc, axis=-1, keepdims=True)
    z = yc * lax.rsqrt(var + EPS) * lng_ref[...] + lnb_ref[...]
    a = z * jax.nn.sigmoid(z)
    yo = _dot(a.astype(BF16), wo_ref[...]) + bo_ref[...]
    o_ref[0] = g_ref[0] * yo


def _conv_branch(u, g, conv_w, conv_b, ln_g, ln_b, w_out, b_out, tm=512, rows=32, lanes=256):
    b, t, _ = u.shape
    tm = min(tm, t)
    halo_blocks = tm // CONV_HALO
    row = lambda i, j: (i, j, 0)
    return pl.pallas_call(
        functools.partial(_conv_kernel, tm=tm, rows=rows, lanes=lanes),
        grid=(b, t // tm),
        in_specs=[
            pl.BlockSpec((1, tm, D_MODEL), row),
            pl.BlockSpec((1, CONV_HALO, D_MODEL),
                         lambda i, j: (i, jnp.maximum(j * halo_blocks - 1, 0), 0)),
            pl.BlockSpec((1, tm, D_MODEL), row),
            _const_spec((CONV_WIDTH, D_MODEL)),
            _const_spec((1, D_MODEL)),
            _const_spec((1, D_MODEL)),
            _const_spec((1, D_MODEL)),
            _const_spec((D_MODEL, D_MODEL)),
            _const_spec((1, D_MODEL)),
        ],
        out_specs=pl.BlockSpec((1, tm, D_MODEL), row),
        out_shape=jax.ShapeDtypeStruct((b, t, D_MODEL), F32),
        scratch_shapes=[
            pltpu.VMEM((tm + CONV_HALO, D_MODEL), F32),
            pltpu.VMEM((tm, D_MODEL), F32),
        ],
        compiler_params=_params(2),
        name="conv",
    )(u, u, g, conv_w, conv_b, ln_g, ln_b, w_out, b_out)


def _attn_kernel(q_ref, k_ref, vt_ref, lq1_ref, lk1_ref, lq2_ref, lk2_ref, hn_ref, o_ref,
                 qs_ref, s_ref, p_ref, cm_ref, m_ref, l_ref, al_ref, acc_ref,
                 *, tq, tk, cw, lam_init):
    i = pl.program_id(2)
    q = q_ref[0]
    lane = lax.broadcasted_iota(jnp.int32, (1, LANES), 1)
    map1 = (lane // (HEAD_DIM // 2)) % 2 == 0
    zero = jnp.zeros_like(q)
    qs_ref[0:tq, :] = jnp.where(map1, q, zero)
    qs_ref[tq:, :] = jnp.where(map1, zero, q)
    m_ref[...] = jnp.full(m_ref.shape, -jnp.inf, F32)
    l_ref[...] = jnp.zeros(l_ref.shape, F32)
    acc_ref[...] = jnp.zeros(acc_ref.shape, F32)

    n_chunks = 2 * tq // cw
    n_diag = tq // tk
    every = tuple(range(n_chunks))

    def q_offset(c):
        return (c * cw) % tq

    def diag_chunks(d):
        return tuple(c for c in every if q_offset(c) + cw > d * tk)

    def scores(j, chunks):
        start = pl.multiple_of(j * tk, tk)
        kb = k_ref[0, pl.ds(start, tk), :]
        for c in chunks:
            s = _dot_nt(kb, qs_ref[c * cw:(c + 1) * cw, :])
            s_ref[c] = s
            cm_ref[c] = jnp.max(s, axis=0, keepdims=True)

    def values(j, chunks):
        start = pl.multiple_of(j * tk, tk)
        vb = vt_ref[0, :, pl.ds(start, tk)]
        for c in chunks:
            acc_ref[c] = acc_ref[c] * al_ref[c] + _dot(vb, p_ref[c])

    def softmax(chunks, diag=None):
        for c in chunks:
            s = s_ref[c]
            if diag is None:
                cm = cm_ref[c]
            else:
                key = lax.broadcasted_iota(jnp.int32, (tk, cw), 0) + diag * tk
                qry = lax.broadcasted_iota(jnp.int32, (tk, cw), 1) + q_offset(c)
                s = jnp.where(key <= qry, s, -jnp.inf)
                cm = jnp.max(s, axis=0, keepdims=True)
            m_old = m_ref[c]
            m_new = jnp.maximum(m_old, cm)
            alpha = jnp.exp2(m_old - m_new)
            p = jnp.exp2(s - m_new)
            l_ref[c] = alpha * l_ref[c] + jnp.sum(p, axis=0, keepdims=True)
            m_ref[c] = m_new
            al_ref[c] = alpha
            p_ref[c] = p.astype(BF16)

    def diagonal_sweep(first, prev_pending):
        for d in range(n_diag):
            if d > 0:
                values(first + d - 1, diag_chunks(d - 1))
            elif prev_pending:
                values(first - 1, every)
            softmax(diag_chunks(d), diag=d)
            if d + 1 < n_diag:
                scores(first + d + 1, diag_chunks(d + 1))
        values(first + n_diag - 1, diag_chunks(n_diag - 1))

    def finalize():
        lam = (jnp.exp(jnp.sum(lq1_ref[...] * lk1_ref[...], axis=-1, keepdims=True))
               - jnp.exp(jnp.sum(lq2_ref[...] * lk2_ref[...], axis=-1, keepdims=True)) + lam_init)
        half = n_chunks // 2
        for c in range(half):
            o1 = acc_ref[c] / l_ref[c]
            o2 = acc_ref[half + c] / l_ref[half + c]
            o = o1 - lam * o2
            ms = jnp.mean(o * o, axis=0, keepdims=True)
            on = (o * lax.rsqrt(ms + EPS)).T
            o_ref[0, c * cw:(c + 1) * cw, :] = on * hn_ref[...] * (1.0 - lam_init)

    scores(0, every)

    @pl.when(i == 0)
    def _():
        diagonal_sweep(0, prev_pending=False)
        finalize()

    @pl.when(i > 0)
    def _():
        n_full = n_diag * i
        softmax(every)
        scores(1, every)

        def body(j, carry):
            values(j - 1, every)
            softmax(every)
            scores(j + 1, every)
            return carry

        lax.fori_loop(1, n_full, body, 0)
        diagonal_sweep(n_full, prev_pending=True)
        finalize()


def _attention(q, k, vt, lq1, lk1, lq2, lk2, head_norm, lam_init, tq=512, tk=512, cw=256):
    b, t, _ = q.shape
    tq = min(tq, t)
    tk = min(tk, tq)
    n_chunks = 2 * tq // cw
    vec = _const_spec((1, HEAD_DIM))
    stat = pltpu.VMEM((n_chunks, 1, cw), F32)
    return pl.pallas_call(
        functools.partial(_attn_kernel, tq=tq, tk=tk, cw=cw, lam_init=lam_init),
        grid=(b, N_HEADS, t // tq),
        in_specs=[
            pl.BlockSpec((1, tq, V_DIM), lambda bi, h, i: (bi, i, h)),
            pl.BlockSpec((1, t, V_DIM), lambda bi, h, i: (bi, 0, h)),
            pl.BlockSpec((1, V_DIM, t), lambda bi, h, i: (bi, h, 0)),
            vec, vec, vec, vec,
            _const_spec((1, V_DIM)),
        ],
        out_specs=pl.BlockSpec((1, tq, V_DIM), lambda bi, h, i: (bi, i, h)),
        out_shape=jax.ShapeDtypeStruct((b, t, D_MODEL), F32),
        scratch_shapes=[
            pltpu.VMEM((2 * tq, V_DIM), BF16),
            pltpu.VMEM((n_chunks, tk, cw), F32),
            pltpu.VMEM((n_chunks, tk, cw), BF16),
            stat, stat, stat, stat,
            pltpu.VMEM((n_chunks, V_DIM, cw), F32),
        ],
        compiler_params=_params(3),
        name="attn",
    )(q, k, vt, lq1, lk1, lq2, lk2, head_norm)


def _out_mlp_kernel(x_ref, yc_ref, att_ref, g_ref, mod_ref, pn1_ref, pre2_ref, pn2_ref,
                    wo_ref, w1_ref, w2_ref, o_ref, *, fc):
    g1 = mod_ref[0, 2:3, :]
    sh2 = mod_ref[0, 3:4, :]
    sc2 = mod_ref[0, 4:5, :]
    g2 = mod_ref[0, 5:6, :]
    y = yc_ref[0] + g_ref[0] * att_ref[0]
    y = _dot(y.astype(BF16), wo_ref[...])
    x1 = x_ref[0] + g1 * _rms(y, pn1_ref[...])
    hb = (_rms(x1, pre2_ref[...]) * (1.0 + sc2) + sh2).astype(BF16)
    f = None
    for c in range(0, D_FF, fc):
        a = jnp.square(jnp.maximum(_dot(hb, w1_ref[:, c:c + fc]), 0.0))
        part = _dot(a.astype(BF16), w2_ref[c:c + fc, :])
        f = part if f is None else f + part
    o_ref[0] = x1 + g2 * _rms(f, pn2_ref[...])


def _out_mlp(x, yc, att, g, mod, post1, pre2, post2, w_o, w1, w2, tm=256, fc=1024):
    b, t, _ = x.shape
    tm = min(tm, t)
    row = lambda i, j: (i, j, 0)
    tile = pl.BlockSpec((1, tm, D_MODEL), row)
    return pl.pallas_call(
        functools.partial(_out_mlp_kernel, fc=fc),
        grid=(b, t // tm),
        in_specs=[
            tile, tile, tile,
            pl.BlockSpec((1, tm, D_MODEL), lambda i, j: (i, j, 1)),
            pl.BlockSpec((1, N_MOD, D_MODEL), lambda i, j: (i, 0, 0)),
            _const_spec((1, D_MODEL)),
            _const_spec((1, D_MODEL)),
            _const_spec((1, D_MODEL)),
            _const_spec((D_MODEL, D_MODEL)),
            _const_spec((D_MODEL, D_FF)),
            _const_spec((D_FF, D_MODEL)),
        ],
        out_specs=tile,
        out_shape=jax.ShapeDtypeStruct((b, t, D_MODEL), F32),
        compiler_params=_params(2),
        name="out_mlp",
    )(x, yc, att, g, mod, post1, pre2, post2, w_o, w1, w2)


def _qk_column_order():
    idx = np.empty((N_HEADS, 4, HEAD_DIM // 2), np.int32)
    for h in range(N_HEADS):
        for seg in range(4):
            m, half = seg % 2, seg // 2
            idx[h, seg] = m * N_HEADS * HEAD_DIM + h * HEAD_DIM + half * (HEAD_DIM // 2) \
                + np.arange(HEAD_DIM // 2)
    return idx.reshape(-1)


def _rope_tables(t):
    half = HEAD_DIM // 2
    inv_freq = ROPE_THETA ** (-jnp.arange(0, HEAD_DIM, 2, dtype=F32) / HEAD_DIM)
    ang = jnp.arange(t).astype(F32)[:, None] * inv_freq[None, :]
    cos, sin = jnp.cos(ang), jnp.sin(ang)
    return jnp.tile(cos, (1, LANES // half)), jnp.concatenate([-sin, -sin, sin, sin], axis=1)


def kernel(x, c, w_ada, b_ada, pre_norm1, post_norm1, w_in, conv_w, conv_b, conv_ln_g, conv_ln_b,
           conv_w_out, conv_b_out, lambda_q1, lambda_k1, lambda_q2, lambda_k2, head_norm, w_o,
           pre_norm2, post_norm2, w_ff1, w_ff2):
    b, t, _ = x.shape
    depth = w_in.shape[0]
    d_qk = 2 * N_HEADS * HEAD_DIM
    o1 = 2 * D_MODEL
    o2 = o1 + d_qk
    o3 = o2 + d_qk
    o4 = o3 + D_MODEL
    order = _qk_column_order()
    cos_t, sin_t = _rope_tables(t)
    c_pad = jnp.pad(c, ((0, (-b) % 8), (0, 0)))
    row = lambda v: v.reshape(1, -1)

    for l in range(depth):
        mod = _modulation(c_pad, w_ada[l], row(b_ada[l]))[:b].reshape(b, N_MOD, D_MODEL)
        wl = w_in[l]
        wglu = wl[:, :o1].astype(BF16)
        wq = jnp.take(wl[:, o1:o2], order, axis=1).astype(BF16)
        wk = jnp.take(wl[:, o2:o3], order, axis=1).astype(BF16)
        wvt = wl[:, o3:o4].T.astype(BF16)
        wg = wl[:, o4:].astype(BF16)
        q_scale = LOG2E / math.sqrt(HEAD_DIM)
        u, q, k, vt, g = _in_proj(x, mod, row(pre_norm1[l]), cos_t, sin_t, wglu, wq, wk, wvt, wg,
                                  q_scale)
        yc = _conv_branch(u, g, conv_w[l], row(conv_b[l]), row(conv_ln_g[l]), row(conv_ln_b[l]),
                          conv_w_out[l].astype(BF16), row(conv_b_out[l]))
        att = _attention(q, k, vt, row(lambda_q1[l]), row(lambda_k1[l]), row(lambda_q2[l]),
                         row(lambda_k2[l]), row(head_norm[l]), _lambda_init(l + 1))
        x = _out_mlp(x, yc, att, g, mod, row(post_norm1[l]), row(pre_norm2[l]), row(post_norm2[l]),
                     w_o[l].astype(BF16), w_ff1[l].astype(BF16), w_ff2[l].astype(BF16))
    return x
```

```python
import functools
import math

import numpy as np
import jax
import jax.numpy as jnp
from jax import lax
from jax.experimental import pallas as pl
from jax.experimental.pallas import tpu as pltpu

F32 = jnp.float32
BF16 = jnp.bfloat16

D_MODEL = 1024
HEAD_DIM = 64
V_DIM = 2 * HEAD_DIM
N_HEADS = D_MODEL // V_DIM
CONV_WIDTH = 31
CONV_HALO = 32
D_FF = 4 * D_MODEL
N_MOD = 6
ROPE_THETA = 10000.0
EPS = 1e-6
LOG2E = 1.4426950408889634
LANES = 128
VMEM_LIMIT = 56 * 1024 * 1024


def _lambda_init(layer_idx):
    return 0.8 - 0.6 * math.exp(-0.3 * (layer_idx - 1))


def _rms(x, g):
    ms = jnp.mean(x * x, axis=-1, keepdims=True)
    return x * lax.rsqrt(ms + EPS) * g


def _dot(a, b):
    return jnp.dot(a, b, preferred_element_type=F32)


def _dot_nt(a, b):
    return lax.dot_general(a, b, (((1,), (1,)), ((), ())), preferred_element_type=F32)


def _const_spec(shape):
    return pl.BlockSpec(shape, lambda *_: (0,) * len(shape), pipeline_mode=pl.Buffered(1))


def _params(n_axes):
    return pltpu.CompilerParams(
        dimension_semantics=("arbitrary",) * n_axes, vmem_limit_bytes=VMEM_LIMIT)


def _mod_kernel(c_ref, w_ref, b_ref, o_ref):
    c = c_ref[...]
    cs = c * jax.nn.sigmoid(c)
    o_ref[...] = _dot(cs.astype(BF16), w_ref[...].astype(BF16)) + b_ref[...]


def _modulation(c_pad, w_ada, b_ada, tn=1024):
    rows = c_pad.shape[0]
    n = w_ada.shape[1]
    return pl.pallas_call(
        _mod_kernel,
        grid=(n // tn,),
        in_specs=[
            pl.BlockSpec((rows, D_MODEL), lambda j: (0, 0)),
            pl.BlockSpec((D_MODEL, tn), lambda j: (0, j)),
            pl.BlockSpec((1, tn), lambda j: (0, j)),
        ],
        out_specs=pl.BlockSpec((rows, tn), lambda j: (0, j)),
        out_shape=jax.ShapeDtypeStruct((rows, n), F32),
        compiler_params=_params(1),
        name="mod",
    )(c_pad, w_ada, b_ada)


def _in_proj_kernel(x_ref, mod_ref, pn_ref, cos_ref, sin_ref, wglu_ref, wq_ref, wk_ref, wvt_ref,
                    wg_ref, u_ref, q_ref, k_ref, vt_ref, g_ref, *, q_scale, cn):
    x = x_ref[0]
    sh1 = mod_ref[0, 0:1, :]
    sc1 = mod_ref[0, 1:2, :]
    h = _rms(x, pn_ref[...]) * (1.0 + sc1) + sh1
    hb = h.astype(BF16)

    for c in range(0, D_MODEL, cn):
        ua = _dot(hb, wglu_ref[:, c:c + cn])
        ub = _dot(hb, wglu_ref[:, D_MODEL + c:D_MODEL + c + cn])
        u_ref[0, :, c:c + cn] = ua * jax.nn.sigmoid(ub)

    cos = cos_ref[...]
    sin = sin_ref[...]
    for w_ref, o_ref, scale in ((wq_ref, q_ref, q_scale), (wk_ref, k_ref, None)):
        for c in range(0, D_MODEL, cn):
            y = _dot(hb, w_ref[:, c:c + cn])
            for g in range(0, cn, LANES):
                yg = y[:, g:g + LANES]
                r = yg * cos + pltpu.roll(yg, LANES // 2, axis=1) * sin
                if scale is not None:
                    r = r * scale
                o_ref[0, :, c + g:c + g + LANES] = r.astype(o_ref.dtype)

    for c in range(0, D_MODEL, cn):
        vt_ref[0, c:c + cn, :] = _dot_nt(wvt_ref[c:c + cn, :], hb).astype(vt_ref.dtype)

    for c in range(0, 2 * D_MODEL, cn):
        g_ref[0, :, c:c + cn] = jax.nn.sigmoid(_dot(hb, wg_ref[:, c:c + cn]))


def _in_proj(x, mod, pre_norm, cos_t, sin_t, wglu, wq, wk, wvt, wg, q_scale, tm=512, cn=512):
    b, t, _ = x.shape
    tm = min(tm, t)
    grid = (b, t // tm)
    row = lambda i, j: (i, j, 0)
    return pl.pallas_call(
        functools.partial(_in_proj_kernel, q_scale=q_scale, cn=cn),
        grid=grid,
        in_specs=[
            pl.BlockSpec((1, tm, D_MODEL), row),
            pl.BlockSpec((1, N_MOD, D_MODEL), lambda i, j: (i, 0, 0)),
            _const_spec((1, D_MODEL)),
            pl.BlockSpec((tm, LANES), lambda i, j: (j, 0)),
            pl.BlockSpec((tm, LANES), lambda i, j: (j, 0)),
            _const_spec((D_MODEL, 2 * D_MODEL)),
            _const_spec((D_MODEL, D_MODEL)),
            _const_spec((D_MODEL, D_MODEL)),
            _const_spec((D_MODEL, D_MODEL)),
            _const_spec((D_MODEL, 2 * D_MODEL)),
        ],
        out_specs=[
            pl.BlockSpec((1, tm, D_MODEL), row),
            pl.BlockSpec((1, tm, D_MODEL), row),
            pl.BlockSpec((1, tm, D_MODEL), row),
            pl.BlockSpec((1, D_MODEL, tm), lambda i, j: (i, 0, j)),
            pl.BlockSpec((1, tm, 2 * D_MODEL), row),
        ],
        out_shape=[
            jax.ShapeDtypeStruct((b, t, D_MODEL), F32),
            jax.ShapeDtypeStruct((b, t, D_MODEL), BF16),
            jax.ShapeDtypeStruct((b, t, D_MODEL), BF16),
            jax.ShapeDtypeStruct((b, D_MODEL, t), BF16),
            jax.ShapeDtypeStruct((b, t, 2 * D_MODEL), F32),
        ],
        compiler_params=_params(2),
        name="in_proj",
    )(x, mod, pre_norm, cos_t, sin_t, wglu, wq, wk, wvt, wg)


def _conv_kernel(u_ref, halo_ref, g_ref, cw_ref, cb_ref, lng_ref, lnb_ref, wo_ref, bo_ref,
                 o_ref, xp_ref, cv_ref, *, tm, rows, lanes):
    t = pl.program_id(1)
    halo = halo_ref[0]
    xp_ref[0:CONV_HALO, :] = jnp.where(t > 0, halo, jnp.zeros_like(halo))
    xp_ref[CONV_HALO:, :] = u_ref[0]

    first = CONV_HALO - (CONV_WIDTH - 1)
    win = rows + CONV_HALO

    def row_chunk(rc, carry):
        base = pl.multiple_of(rc * rows, rows)
        for lc in range(0, D_MODEL, lanes):
            w = xp_ref[pl.ds(base, win), lc:lc + lanes]
            acc = jnp.zeros((rows, lanes), F32)
            for j in range(CONV_WIDTH):
                shifted = pltpu.roll(w, win - (first + j), axis=0)[:rows]
                acc = acc + shifted * cw_ref[j:j + 1, lc:lc + lanes]
            cv_ref[pl.ds(base, rows), lc:lc + lanes] = acc
        return carry

    lax.fori_loop(0, tm // rows, row_chunk, 0)

    y = cv_ref[...] + cb_ref[...]
    mu = jnp.mean(y, axis=-1, keepdims=True)
    yc = y - mu
    var = jnp.mean(yc * yc, axis=-1, keepdims=True)
    z = yc * lax.rsqrt(var + EPS) * lng_ref[...] + lnb_ref[...]
    a = z * jax.nn.sigmoid(z)
    yo = _dot(a.astype(BF16), wo_ref[...]) + bo_ref[...]
    o_ref[0] = g_ref[0] * yo


def _conv_branch(u, g, conv_w, conv_b, ln_g, ln_b, w_out, b_out, tm=512, rows=32, lanes=256):
    b, t, _ = u.shape
    tm = min(tm, t)
    halo_blocks = tm // CONV_HALO
    row = lambda i, j: (i, j, 0)
    return pl.pallas_call(
        functools.partial(_conv_kernel, tm=tm, rows=rows, lanes=lanes),
        grid=(b, t // tm),
        in_specs=[
            pl.BlockSpec((1, tm, D_MODEL), row),
            pl.BlockSpec((1, CONV_HALO, D_MODEL),
                         lambda i, j: (i, jnp.maximum(j * halo_blocks - 1, 0), 0)),
            pl.BlockSpec((1, tm, D_MODEL), row),
            _const_spec((CONV_WIDTH, D_MODEL)),
            _const_spec((1, D_MODEL)),
            _const_spec((1, D_MODEL)),
            _const_spec((1, D_MODEL)),
            _const_spec((D_MODEL, D_MODEL)),
            _const_spec((1, D_MODEL)),
        ],
        out_specs=pl.BlockSpec((1, tm, D_MODEL), row),
        out_shape=jax.ShapeDtypeStruct((b, t, D_MODEL), F32),
        scratch_shapes=[
            pltpu.VMEM((tm + CONV_HALO, D_MODEL), F32),
            pltpu.VMEM((tm, D_MODEL), F32),
        ],
        compiler_params=_params(2),
        name="conv",
    )(u, u, g, conv_w, conv_b, ln_g, ln_b, w_out, b_out)


def _attn_kernel(q_ref, k_ref, vt_ref, lq1_ref, lk1_ref, lq2_ref, lk2_ref, hn_ref, o_ref,
                 qs_ref, s_ref, p_ref, cm_ref, m_ref, l_ref, al_ref, acc_ref,
                 *, tq, tk, cw, lam_init):
    i = pl.program_id(2)
    q = q_ref[0]
    lane = lax.broadcasted_iota(jnp.int32, (1, LANES), 1)
    map1 = (lane // (HEAD_DIM // 2)) % 2 == 0
    zero = jnp.zeros_like(q)
    qs_ref[0:tq, :] = jnp.where(map1, q, zero)
    qs_ref[tq:, :] = jnp.where(map1, zero, q)
    m_ref[...] = jnp.full(m_ref.shape, -jnp.inf, F32)
    l_ref[...] = jnp.zeros(l_ref.shape, F32)
    acc_ref[...] = jnp.zeros(acc_ref.shape, F32)

    n_chunks = 2 * tq // cw
    n_diag = tq // tk
    every = tuple(range(n_chunks))

    def q_offset(c):
        return (c * cw) % tq

    def diag_chunks(d):
        return tuple(c for c in every if q_offset(c) + cw > d * tk)

    def scores(j, chunks):
        start = pl.multiple_of(j * tk, tk)
        kb = k_ref[0, pl.ds(start, tk), :]
        for c in chunks:
            s = _dot_nt(kb, qs_ref[c * cw:(c + 1) * cw, :])
            s_ref[c] = s
            cm_ref[c] = jnp.max(s, axis=0, keepdims=True)

    def values(j, chunks):
        start = pl.multiple_of(j * tk, tk)
        vb = vt_ref[0, :, pl.ds(start, tk)]
        for c in chunks:
            acc_ref[c] = acc_ref[c] * al_ref[c] + _dot(vb, p_ref[:, c * cw:(c + 1) * cw])

    def softmax(chunks, diag=None):
        for c in chunks:
            s = s_ref[c]
            if diag is None:
                cm = cm_ref[c]
            else:
                key = lax.broadcasted_iota(jnp.int32, (tk, cw), 0) + diag * tk
                qry = lax.broadcasted_iota(jnp.int32, (tk, cw), 1) + q_offset(c)
                s = jnp.where(key <= qry, s, -jnp.inf)
                cm = jnp.max(s, axis=0, keepdims=True)
            m_old = m_ref[c]
            m_new = jnp.maximum(m_old, cm)
            alpha = jnp.exp2(m_old - m_new)
            p = jnp.exp2(s - m_new)
            l_ref[c] = alpha * l_ref[c] + jnp.sum(p, axis=0, keepdims=True)
            m_ref[c] = m_new
            al_ref[c] = alpha
            p_ref[:, c * cw:(c + 1) * cw] = p.astype(BF16)

    def diagonal_sweep(first, prev_pending):
        for d in range(n_diag):
            if d > 0:
                values(first + d - 1, diag_chunks(d - 1))
            elif prev_pending:
                values(first - 1, every)
            softmax(diag_chunks(d), diag=d)
            if d + 1 < n_diag:
                scores(first + d + 1, diag_chunks(d + 1))
        values(first + n_diag - 1, diag_chunks(n_diag - 1))

    def finalize():
        lam = (jnp.exp(jnp.sum(lq1_ref[...] * lk1_ref[...], axis=-1, keepdims=True))
               - jnp.exp(jnp.sum(lq2_ref[...] * lk2_ref[...], axis=-1, keepdims=True)) + lam_init)
        half = n_chunks // 2
        for c in range(half):
            o1 = acc_ref[c] / l_ref[c]
            o2 = acc_ref[half + c] / l_ref[half + c]
            o = o1 - lam * o2
            ms = jnp.mean(o * o, axis=0, keepdims=True)
            on = (o * lax.rsqrt(ms + EPS)).T
            o_ref[0, c * cw:(c + 1) * cw, :] = on * hn_ref[...] * (1.0 - lam_init)

    scores(0, every)

    @pl.when(i == 0)
    def _():
        diagonal_sweep(0, prev_pending=False)
        finalize()

    @pl.when(i > 0)
    def _():
        n_full = n_diag * i
        softmax(every)
        scores(1, every)

        def body(j, carry):
            values(j - 1, every)
            softmax(every)
            scores(j + 1, every)
            return carry

        lax.fori_loop(1, n_full, body, 0)
        diagonal_sweep(n_full, prev_pending=True)
        finalize()


def _attention(q, k, vt, lq1, lk1, lq2, lk2, head_norm, lam_init, tq=512, tk=512, cw=256):
    b, t, _ = q.shape
    tq = min(tq, t)
    tk = min(tk, tq)
    n_chunks = 2 * tq // cw
    vec = _const_spec((1, HEAD_DIM))
    stat = pltpu.VMEM((n_chunks, 1, cw), F32)
    return pl.pallas_call(
        functools.partial(_attn_kernel, tq=tq, tk=tk, cw=cw, lam_init=lam_init),
        grid=(b, N_HEADS, t // tq),
        in_specs=[
            pl.BlockSpec((1, tq, V_DIM), lambda bi, h, i: (bi, i, h)),
            pl.BlockSpec((1, t, V_DIM), lambda bi, h, i: (bi, 0, h)),
            pl.BlockSpec((1, V_DIM, t), lambda bi, h, i: (bi, h, 0)),
            vec, vec, vec, vec,
            _const_spec((1, V_DIM)),
        ],
        out_specs=pl.BlockSpec((1, tq, V_DIM), lambda bi, h, i: (bi, i, h)),
        out_shape=jax.ShapeDtypeStruct((b, t, D_MODEL), F32),
        scratch_shapes=[
            pltpu.VMEM((2 * tq, V_DIM), BF16),
            pltpu.VMEM((n_chunks, tk, cw), F32),
            pltpu.VMEM((tk, 2 * tq), BF16),
            stat, stat, stat, stat,
            pltpu.VMEM((n_chunks, V_DIM, cw), F32),
        ],
        compiler_params=_params(3),
        name="attn",
    )(q, k, vt, lq1, lk1, lq2, lk2, head_norm)


def _out_mlp_kernel(x_ref, yc_ref, att_ref, g_ref, mod_ref, pn1_ref, pre2_ref, pn2_ref,
                    wo_ref, w1_ref, w2_ref, o_ref, *, fc):
    g1 = mod_ref[0, 2:3, :]
    sh2 = mod_ref[0, 3:4, :]
    sc2 = mod_ref[0, 4:5, :]
    g2 = mod_ref[0, 5:6, :]
    y = yc_ref[0] + g_ref[0] * att_ref[0]
    y = _dot(y.astype(BF16), wo_ref[...])
    x1 = x_ref[0] + g1 * _rms(y, pn1_ref[...])
    hb = (_rms(x1, pre2_ref[...]) * (1.0 + sc2) + sh2).astype(BF16)
    f = None
    for c in range(0, D_FF, fc):
        a = jnp.square(jnp.maximum(_dot(hb, w1_ref[:, c:c + fc]), 0.0))
        part = _dot(a.astype(BF16), w2_ref[c:c + fc, :])
        f = part if f is None else f + part
    o_ref[0] = x1 + g2 * _rms(f, pn2_ref[...])


def _out_mlp(x, yc, att, g, mod, post1, pre2, post2, w_o, w1, w2, tm=512, fc=1024):
    b, t, _ = x.shape
    tm = min(tm, t)
    row = lambda i, j: (i, j, 0)
    tile = pl.BlockSpec((1, tm, D_MODEL), row)
    return pl.pallas_call(
        functools.partial(_out_mlp_kernel, fc=fc),
        grid=(b, t // tm),
        in_specs=[
            tile, tile, tile,
            pl.BlockSpec((1, tm, D_MODEL), lambda i, j: (i, j, 1)),
            pl.BlockSpec((1, N_MOD, D_MODEL), lambda i, j: (i, 0, 0)),
            _const_spec((1, D_MODEL)),
            _const_spec((1, D_MODEL)),
            _const_spec((1, D_MODEL)),
            _const_spec((D_MODEL, D_MODEL)),
            _const_spec((D_MODEL, D_FF)),
            _const_spec((D_FF, D_MODEL)),
        ],
        out_specs=tile,
        out_shape=jax.ShapeDtypeStruct((b, t, D_MODEL), F32),
        compiler_params=_params(2),
        name="out_mlp",
    )(x, yc, att, g, mod, post1, pre2, post2, w_o, w1, w2)


def _qk_column_order():
    idx = np.empty((N_HEADS, 4, HEAD_DIM // 2), np.int32)
    for h in range(N_HEADS):
        for seg in range(4):
            m, half = seg % 2, seg // 2
            idx[h, seg] = m * N_HEADS * HEAD_DIM + h * HEAD_DIM + half * (HEAD_DIM // 2) \
                + np.arange(HEAD_DIM // 2)
    return idx.reshape(-1)


def _rope_tables(t):
    half = HEAD_DIM // 2
    inv_freq = ROPE_THETA ** (-jnp.arange(0, HEAD_DIM, 2, dtype=F32) / HEAD_DIM)
    ang = jnp.arange(t).astype(F32)[:, None] * inv_freq[None, :]
    cos, sin = jnp.cos(ang), jnp.sin(ang)
    return jnp.tile(cos, (1, LANES // half)), jnp.concatenate([-sin, -sin, sin, sin], axis=1)


def kernel(x, c, w_ada, b_ada, pre_norm1, post_norm1, w_in, conv_w, conv_b, conv_ln_g, conv_ln_b,
           conv_w_out, conv_b_out, lambda_q1, lambda_k1, lambda_q2, lambda_k2, head_norm, w_o,
           pre_norm2, post_norm2, w_ff1, w_ff2):
    b, t, _ = x.shape
    depth = w_in.shape[0]
    d_qk = 2 * N_HEADS * HEAD_DIM
    o1 = 2 * D_MODEL
    o2 = o1 + d_qk
    o3 = o2 + d_qk
    o4 = o3 + D_MODEL
    order = _qk_column_order()
    cos_t, sin_t = _rope_tables(t)
    c_pad = jnp.pad(c, ((0, (-b) % 8), (0, 0)))
    row = lambda v: v.reshape(1, -1)

    for l in range(depth):
        mod = _modulation(c_pad, w_ada[l], row(b_ada[l]))[:b].reshape(b, N_MOD, D_MODEL)
        wl = w_in[l]
        wglu = wl[:, :o1].astype(BF16)
        wq = jnp.take(wl[:, o1:o2], order, axis=1).astype(BF16)
        wk = jnp.take(wl[:, o2:o3], order, axis=1).astype(BF16)
        wvt = wl[:, o3:o4].T.astype(BF16)
        wg = wl[:, o4:].astype(BF16)
        q_scale = LOG2E / math.sqrt(HEAD_DIM)
        u, q, k, vt, g = _in_proj(x, mod, row(pre_norm1[l]), cos_t, sin_t, wglu, wq, wk, wvt, wg,
                                  q_scale)
        yc = _conv_branch(u, g, conv_w[l], row(conv_b[l]), row(conv_ln_g[l]), row(conv_ln_b[l]),
                          conv_w_out[l].astype(BF16), row(conv_b_out[l]))
        att = _attention(q, k, vt, row(lambda_q1[l]), row(lambda_k1[l]), row(lambda_q2[l]),
                         row(lambda_k2[l]), row(head_norm[l]), _lambda_init(l + 1))
        x = _out_mlp(x, yc, att, g, mod, row(post_norm1[l]), row(pre_norm2[l]), row(post_norm2[l]),
                     w_o[l].astype(BF16), w_ff1[l].astype(BF16), w_ff2[l].astype(BF16))
    return x
```

```python
import functools
import math

import numpy as np
import jax
import jax.numpy as jnp
from jax import lax
from jax.experimental import pallas as pl
from jax.experimental.pallas import tpu as pltpu

F32 = jnp.float32
BF16 = jnp.bfloat16

D_MODEL = 1024
HEAD_DIM = 64
V_DIM = 2 * HEAD_DIM
N_HEADS = D_MODEL // V_DIM
CONV_WIDTH = 31
CONV_HALO = 32
D_FF = 4 * D_MODEL
N_MOD = 6
ROPE_THETA = 10000.0
EPS = 1e-6
LOG2E = 1.4426950408889634
LANES = 128
VMEM_LIMIT = 56 * 1024 * 1024


def _lambda_init(layer_idx):
    return 0.8 - 0.6 * math.exp(-0.3 * (layer_idx - 1))


def _rms(x, g):
    ms = jnp.mean(x * x, axis=-1, keepdims=True)
    return x * lax.rsqrt(ms + EPS) * g


def _dot(a, b):
    return jnp.dot(a, b, preferred_element_type=F32)


def _dot_nt(a, b):
    return lax.dot_general(a, b, (((1,), (1,)), ((), ())), preferred_element_type=F32)


def _const_spec(shape):
    return pl.BlockSpec(shape, lambda *_: (0,) * len(shape), pipeline_mode=pl.Buffered(1))


def _params(n_axes):
    return pltpu.CompilerParams(
        dimension_semantics=("arbitrary",) * n_axes, vmem_limit_bytes=VMEM_LIMIT)


def _mod_kernel(c_ref, w_ref, b_ref, o_ref):
    c = c_ref[...]
    cs = c * jax.nn.sigmoid(c)
    o_ref[...] = _dot(cs.astype(BF16), w_ref[...].astype(BF16)) + b_ref[...]


def _modulation(c_pad, w_ada, b_ada, tn=1024):
    rows = c_pad.shape[0]
    n = w_ada.shape[1]
    return pl.pallas_call(
        _mod_kernel,
        grid=(n // tn,),
        in_specs=[
            pl.BlockSpec((rows, D_MODEL), lambda j: (0, 0)),
            pl.BlockSpec((D_MODEL, tn), lambda j: (0, j)),
            pl.BlockSpec((1, tn), lambda j: (0, j)),
        ],
        out_specs=pl.BlockSpec((rows, tn), lambda j: (0, j)),
        out_shape=jax.ShapeDtypeStruct((rows, n), F32),
        compiler_params=_params(1),
        name="mod",
    )(c_pad, w_ada, b_ada)


def _in_proj_kernel(x_ref, mod_ref, pn_ref, cos_ref, sin_ref, wglu_ref, wq_ref, wk_ref, wvt_ref,
                    wg_ref, u_ref, q_ref, k_ref, vt_ref, g_ref, *, q_scale, cn):
    x = x_ref[0]
    sh1 = mod_ref[0, 0:1, :]
    sc1 = mod_ref[0, 1:2, :]
    h = _rms(x, pn_ref[...]) * (1.0 + sc1) + sh1
    hb = h.astype(BF16)

    for c in range(0, D_MODEL, cn):
        ua = _dot(hb, wglu_ref[:, c:c + cn])
        ub = _dot(hb, wglu_ref[:, D_MODEL + c:D_MODEL + c + cn])
        u_ref[0, :, c:c + cn] = ua * jax.nn.sigmoid(ub)

    cos = cos_ref[...]
    sin = sin_ref[...]
    for w_ref, o_ref, scale in ((wq_ref, q_ref, q_scale), (wk_ref, k_ref, None)):
        for c in range(0, D_MODEL, cn):
            y = _dot(hb, w_ref[:, c:c + cn])
            for g in range(0, cn, LANES):
                yg = y[:, g:g + LANES]
                r = yg * cos + pltpu.roll(yg, LANES // 2, axis=1) * sin
                if scale is not None:
                    r = r * scale
                o_ref[0, :, c + g:c + g + LANES] = r.astype(o_ref.dtype)

    for c in range(0, D_MODEL, cn):
        vt_ref[0, c:c + cn, :] = _dot_nt(wvt_ref[c:c + cn, :], hb).astype(vt_ref.dtype)

    for c in range(0, 2 * D_MODEL, cn):
        g_ref[0, :, c:c + cn] = jax.nn.sigmoid(_dot(hb, wg_ref[:, c:c + cn]))


def _in_proj(x, mod, pre_norm, cos_t, sin_t, wglu, wq, wk, wvt, wg, q_scale, tm=512, cn=512):
    b, t, _ = x.shape
    tm = min(tm, t)
    grid = (b, t // tm)
    row = lambda i, j: (i, j, 0)
    return pl.pallas_call(
        functools.partial(_in_proj_kernel, q_scale=q_scale, cn=cn),
        grid=grid,
        in_specs=[
            pl.BlockSpec((1, tm, D_MODEL), row),
            pl.BlockSpec((1, N_MOD, D_MODEL), lambda i, j: (i, 0, 0)),
            _const_spec((1, D_MODEL)),
            pl.BlockSpec((tm, LANES), lambda i, j: (j, 0)),
            pl.BlockSpec((tm, LANES), lambda i, j: (j, 0)),
            _const_spec((D_MODEL, 2 * D_MODEL)),
            _const_spec((D_MODEL, D_MODEL)),
            _const_spec((D_MODEL, D_MODEL)),
            _const_spec((D_MODEL, D_MODEL)),
            _const_spec((D_MODEL, 2 * D_MODEL)),
        ],
        out_specs=[
            pl.BlockSpec((1, tm, D_MODEL), row),
            pl.BlockSpec((1, tm, D_MODEL), row),
            pl.BlockSpec((1, tm, D_MODEL), row),
            pl.BlockSpec((1, D_MODEL, tm), lambda i, j: (i, 0, j)),
            pl.BlockSpec((1, tm, 2 * D_MODEL), row),
        ],
        out_shape=[
            jax.ShapeDtypeStruct((b, t, D_MODEL), F32),
            jax.ShapeDtypeStruct((b, t, D_MODEL), BF16),
            jax.ShapeDtypeStruct((b, t, D_MODEL), BF16),
            jax.ShapeDtypeStruct((b, D_MODEL, t), BF16),
            jax.ShapeDtypeStruct((b, t, 2 * D_MODEL), F32),
        ],
        compiler_params=_params(2),
        name="in_proj",
    )(x, mod, pre_norm, cos_t, sin_t, wglu, wq, wk, wvt, wg)


def _conv_kernel(u_ref, halo_ref, g_ref, cw_ref, cb_ref, lng_ref, lnb_ref, wo_ref, bo_ref,
                 o_ref, xp_ref, cv_ref, *, tm, rows, lanes):
    t = pl.program_id(1)
    halo = halo_ref[0]
    xp_ref[0:CONV_HALO, :] = jnp.where(t > 0, halo, jnp.zeros_like(halo))
    xp_ref[CONV_HALO:, :] = u_ref[0]

    first = CONV_HALO - (CONV_WIDTH - 1)
    win = rows + CONV_HALO

    def row_chunk(rc, carry):
        base = pl.multiple_of(rc * rows, rows)
        for lc in range(0, D_MODEL, lanes):
            w = xp_ref[pl.ds(base, win), lc:lc + lanes]
            acc = jnp.zeros((rows, lanes), F32)
            for j in range(CONV_WIDTH):
                shifted = pltpu.roll(w, win - (first + j), axis=0)[:rows]
                acc = acc + shifted * cw_ref[j:j + 1, lc:lc + lanes]
            cv_ref[pl.ds(base, rows), lc:lc + lanes] = acc
        return carry

    lax.fori_loop(0, tm // rows, row_chunk, 0)

    y = cv_ref[...] + cb_ref[...]
    mu = jnp.mean(y, axis=-1, keepdims=True)
    yc = y - mu
    var = jnp.mean(yc * yc, axis=-1, keepdims=True)
    z = yc * lax.rsqrt(var + EPS) * lng_ref[...] + lnb_ref[...]
    a = z * jax.nn.sigmoid(z)
    yo = _dot(a.astype(BF16), wo_ref[...]) + bo_ref[...]
    o_ref[0] = g_ref[0] * yo


def _conv_branch(u, g, conv_w, conv_b, ln_g, ln_b, w_out, b_out, tm=512, rows=64, lanes=256):
    b, t, _ = u.shape
    tm = min(tm, t)
    halo_blocks = tm // CONV_HALO
    row = lambda i, j: (i, j, 0)
    return pl.pallas_call(
        functools.partial(_conv_kernel, tm=tm, rows=rows, lanes=lanes),
        grid=(b, t // tm),
        in_specs=[
            pl.BlockSpec((1, tm, D_MODEL), row),
            pl.BlockSpec((1, CONV_HALO, D_MODEL),
                         lambda i, j: (i, jnp.maximum(j * halo_blocks - 1, 0), 0)),
            pl.BlockSpec((1, tm, D_MODEL), row),
            _const_spec((CONV_WIDTH, D_MODEL)),
            _const_spec((1, D_MODEL)),
            _const_spec((1, D_MODEL)),
            _const_spec((1, D_MODEL)),
            _const_spec((D_MODEL, D_MODEL)),
            _const_spec((1, D_MODEL)),
        ],
        out_specs=pl.BlockSpec((1, tm, D_MODEL), row),
        out_shape=jax.ShapeDtypeStruct((b, t, D_MODEL), F32),
        scratch_shapes=[
            pltpu.VMEM((tm + CONV_HALO, D_MODEL), F32),
            pltpu.VMEM((tm, D_MODEL), F32),
        ],
        compiler_params=_params(2),
        name="conv",
    )(u, u, g, conv_w, conv_b, ln_g, ln_b, w_out, b_out)


def _attn_kernel(q_ref, k_ref, vt_ref, lq1_ref, lk1_ref, lq2_ref, lk2_ref, hn_ref, o_ref,
                 qs_ref, s_ref, p_ref, cm_ref, m_ref, l_ref, al_ref, acc_ref,
                 *, tq, tk, cw, lam_init):
    i = pl.program_id(2)
    q = q_ref[0]
    lane = lax.broadcasted_iota(jnp.int32, (1, LANES), 1)
    map1 = (lane // (HEAD_DIM // 2)) % 2 == 0
    zero = jnp.zeros_like(q)
    qs_ref[0:tq, :] = jnp.where(map1, q, zero)
    qs_ref[tq:, :] = jnp.where(map1, zero, q)
    m_ref[...] = jnp.full(m_ref.shape, -jnp.inf, F32)
    l_ref[...] = jnp.zeros(l_ref.shape, F32)
    acc_ref[...] = jnp.zeros(acc_ref.shape, F32)

    n_chunks = 2 * tq // cw
    n_diag = tq // tk
    every = tuple(range(n_chunks))

    def q_offset(c):
        return (c * cw) % tq

    def diag_chunks(d):
        return tuple(c for c in every if q_offset(c) + cw > d * tk)

    def scores(j, chunks):
        start = pl.multiple_of(j * tk, tk)
        kb = k_ref[0, pl.ds(start, tk), :]
        for c in chunks:
            s = _dot_nt(kb, qs_ref[c * cw:(c + 1) * cw, :])
            s_ref[c] = s
            cm_ref[c] = jnp.max(s, axis=0, keepdims=True)

    def values(j, chunks):
        start = pl.multiple_of(j * tk, tk)
        vb = vt_ref[0, :, pl.ds(start, tk)]
        for c in chunks:
            acc_ref[c] = acc_ref[c] * al_ref[c] + _dot(vb, p_ref[:, c * cw:(c + 1) * cw])

    def softmax(chunks, diag=None):
        for c in chunks:
            s = s_ref[c]
            if diag is None:
                cm = cm_ref[c]
            else:
                key = lax.broadcasted_iota(jnp.int32, (tk, cw), 0) + diag * tk
                qry = lax.broadcasted_iota(jnp.int32, (tk, cw), 1) + q_offset(c)
                s = jnp.where(key <= qry, s, -jnp.inf)
                cm = jnp.max(s, axis=0, keepdims=True)
            m_old = m_ref[c]
            m_new = jnp.maximum(m_old, cm)
            alpha = jnp.exp2(m_old - m_new)
            p = jnp.exp2(s - m_new)
            l_ref[c] = alpha * l_ref[c] + jnp.sum(p, axis=0, keepdims=True)
            m_ref[c] = m_new
            al_ref[c] = alpha
            p_ref[:, c * cw:(c + 1) * cw] = p.astype(BF16)

    def diagonal_sweep(first, prev_pending):
        for d in range(n_diag):
            if d > 0:
                values(first + d - 1, diag_chunks(d - 1))
            elif prev_pending:
                values(first - 1, every)
            softmax(diag_chunks(d), diag=d)
            if d + 1 < n_diag:
                scores(first + d + 1, diag_chunks(d + 1))
        values(first + n_diag - 1, diag_chunks(n_diag - 1))

    def finalize():
        lam = (jnp.exp(jnp.sum(lq1_ref[...] * lk1_ref[...], axis=-1, keepdims=True))
               - jnp.exp(jnp.sum(lq2_ref[...] * lk2_ref[...], axis=-1, keepdims=True)) + lam_init)
        half = n_chunks // 2
        for c in range(half):
            o1 = acc_ref[c] / l_ref[c]
            o2 = acc_ref[half + c] / l_ref[half + c]
            o = o1 - lam * o2
            ms = jnp.mean(o * o, axis=0, keepdims=True)
            on = (o * lax.rsqrt(ms + EPS)).T
            o_ref[0, c * cw:(c + 1) * cw, :] = on * hn_ref[...] * (1.0 - lam_init)

    scores(0, every)

    @pl.when(i == 0)
    def _():
        diagonal_sweep(0, prev_pending=False)
        finalize()

    @pl.when(i > 0)
    def _():
        n_full = n_diag * i
        softmax(every)
        scores(1, every)

        def body(j, carry):
            values(j - 1, every)
            softmax(every)
            scores(j + 1, every)
            return carry

        lax.fori_loop(1, n_full, body, 0)
        diagonal_sweep(n_full, prev_pending=True)
        finalize()


def _attention(q, k, vt, lq1, lk1, lq2, lk2, head_norm, lam_init, tq=512, tk=512, cw=256):
    b, t, _ = q.shape
    tq = min(tq, t)
    tk = min(tk, tq)
    n_chunks = 2 * tq // cw
    vec = _const_spec((1, HEAD_DIM))
    stat = pltpu.VMEM((n_chunks, 1, cw), F32)
    return pl.pallas_call(
        functools.partial(_attn_kernel, tq=tq, tk=tk, cw=cw, lam_init=lam_init),
        grid=(b, N_HEADS, t // tq),
        in_specs=[
            pl.BlockSpec((1, tq, V_DIM), lambda bi, h, i: (bi, i, h)),
            pl.BlockSpec((1, t, V_DIM), lambda bi, h, i: (bi, 0, h)),
            pl.BlockSpec((1, V_DIM, t), lambda bi, h, i: (bi, h, 0)),
            vec, vec, vec, vec,
            _const_spec((1, V_DIM)),
        ],
        out_specs=pl.BlockSpec((1, tq, V_DIM), lambda bi, h, i: (bi, i, h)),
        out_shape=jax.ShapeDtypeStruct((b, t, D_MODEL), F32),
        scratch_shapes=[
            pltpu.VMEM((2 * tq, V_DIM), BF16),
            pltpu.VMEM((n_chunks, tk, cw), F32),
            pltpu.VMEM((tk, 2 * tq), BF16),
            stat, stat, stat, stat,
            pltpu.VMEM((n_chunks, V_DIM, cw), F32),
        ],
        compiler_params=_params(3),
        name="attn",
    )(q, k, vt, lq1, lk1, lq2, lk2, head_norm)


def _out_mlp_kernel(x_ref, yc_ref, att_ref, g_ref, mod_ref, pn1_ref, pre2_ref, pn2_ref,
                    wo_ref, w1_ref, w2_ref, o_ref, *, fc):
    g1 = mod_ref[0, 2:3, :]
    sh2 = mod_ref[0, 3:4, :]
    sc2 = mod_ref[0, 4:5, :]
    g2 = mod_ref[0, 5:6, :]
    y = yc_ref[0] + g_ref[0] * att_ref[0]
    y = _dot(y.astype(BF16), wo_ref[...])
    x1 = x_ref[0] + g1 * _rms(y, pn1_ref[...])
    hb = (_rms(x1, pre2_ref[...]) * (1.0 + sc2) + sh2).astype(BF16)
    f = None
    for c in range(0, D_FF, fc):
        a = jnp.square(jnp.maximum(_dot(hb, w1_ref[:, c:c + fc]), 0.0))
        part = _dot(a.astype(BF16), w2_ref[c:c + fc, :])
        f = part if f is None else f + part
    o_ref[0] = x1 + g2 * _rms(f, pn2_ref[...])


def _out_mlp(x, yc, att, g, mod, post1, pre2, post2, w_o, w1, w2, tm=512, fc=1024):
    b, t, _ = x.shape
    tm = min(tm, t)
    row = lambda i, j: (i, j, 0)
    tile = pl.BlockSpec((1, tm, D_MODEL), row)
    return pl.pallas_call(
        functools.partial(_out_mlp_kernel, fc=fc),
        grid=(b, t // tm),
        in_specs=[
            tile, tile, tile,
            pl.BlockSpec((1, tm, D_MODEL), lambda i, j: (i, j, 1)),
            pl.BlockSpec((1, N_MOD, D_MODEL), lambda i, j: (i, 0, 0)),
            _const_spec((1, D_MODEL)),
            _const_spec((1, D_MODEL)),
            _const_spec((1, D_MODEL)),
            _const_spec((D_MODEL, D_MODEL)),
            _const_spec((D_MODEL, D_FF)),
            _const_spec((D_FF, D_MODEL)),
        ],
        out_specs=tile,
        out_shape=jax.ShapeDtypeStruct((b, t, D_MODEL), F32),
        compiler_params=_params(2),
        name="out_mlp",
    )(x, yc, att, g, mod, post1, pre2, post2, w_o, w1, w2)


def _qk_column_order():
    idx = np.empty((N_HEADS, 4, HEAD_DIM // 2), np.int32)
    for h in range(N_HEADS):
        for seg in range(4):
            m, half = seg % 2, seg // 2
            idx[h, seg] = m * N_HEADS * HEAD_DIM + h * HEAD_DIM + half * (HEAD_DIM // 2) \
                + np.arange(HEAD_DIM // 2)
    return idx.reshape(-1)


def _rope_tables(t):
    half = HEAD_DIM // 2
    inv_freq = ROPE_THETA ** (-jnp.arange(0, HEAD_DIM, 2, dtype=F32) / HEAD_DIM)
    ang = jnp.arange(t).astype(F32)[:, None] * inv_freq[None, :]
    cos, sin = jnp.cos(ang), jnp.sin(ang)
    return jnp.tile(cos, (1, LANES // half)), jnp.concatenate([-sin, -sin, sin, sin], axis=1)


def kernel(x, c, w_ada, b_ada, pre_norm1, post_norm1, w_in, conv_w, conv_b, conv_ln_g, conv_ln_b,
           conv_w_out, conv_b_out, lambda_q1, lambda_k1, lambda_q2, lambda_k2, head_norm, w_o,
           pre_norm2, post_norm2, w_ff1, w_ff2):
    b, t, _ = x.shape
    depth = w_in.shape[0]
    d_qk = 2 * N_HEADS * HEAD_DIM
    o1 = 2 * D_MODEL
    o2 = o1 + d_qk
    o3 = o2 + d_qk
    o4 = o3 + D_MODEL
    order = _qk_column_order()
    cos_t, sin_t = _rope_tables(t)
    c_pad = jnp.pad(c, ((0, (-b) % 8), (0, 0)))
    row = lambda v: v.reshape(1, -1)

    for l in range(depth):
        mod = _modulation(c_pad, w_ada[l], row(b_ada[l]))[:b].reshape(b, N_MOD, D_MODEL)
        wl = w_in[l]
        wglu = wl[:, :o1].astype(BF16)
        wq = jnp.take(wl[:, o1:o2], order, axis=1).astype(BF16)
        wk = jnp.take(wl[:, o2:o3], order, axis=1).astype(BF16)
        wvt = wl[:, o3:o4].T.astype(BF16)
        wg = wl[:, o4:].astype(BF16)
        q_scale = LOG2E / math.sqrt(HEAD_DIM)
        u, q, k, vt, g = _in_proj(x, mod, row(pre_norm1[l]), cos_t, sin_t, wglu, wq, wk, wvt, wg,
                                  q_scale)
        yc = _conv_branch(u, g, conv_w[l], row(conv_b[l]), row(conv_ln_g[l]), row(conv_ln_b[l]),
                          conv_w_out[l].astype(BF16), row(conv_b_out[l]))
        att = _attention(q, k, vt, row(lambda_q1[l]), row(lambda_k1[l]), row(lambda_q2[l]),
                         row(lambda_k2[l]), row(head_norm[l]), _lambda_init(l + 1))
        x = _out_mlp(x, yc, att, g, mod, row(post_norm1[l]), row(pre_norm2[l]), row(post_norm2[l]),
                     w_o[l].astype(BF16), w_ff1[l].astype(BF16), w_ff2[l].astype(BF16))
    return x
```
